```python
import math
import jax, jax.numpy as jnp
from jax import lax
import numpy as np


D_MODEL = 1024
BATCH = 16
SEQ = 4096
DEPTH = 4
DEC_BATCH = 16
DEC_SEQ = 32
PAST_LEN = 4096

CHUNK = 64
D_ATT = D_MODEL // 2
D_SSM = D_MODEL // 2
D_MIX = D_ATT + D_SSM
N_HEADS = D_ATT // 64
HEAD_DIM = 64
SSM_GROUP = 16
N_SSM_GROUPS = D_SSM // SSM_GROUP
STATE_DIM = 64
Q_BLOCK = 128
EPS = 1e-6
D_IN = 4 * D_ATT + 2 * D_SSM

kernel_name = 'hymba_s5_stickbreaking_stream_step'


def _rmsnorm(x, g):
    xf = x.astype(jnp.float32)
    y = xf * lax.rsqrt(jnp.mean(xf * xf, axis=-1, keepdims=True) + EPS) * g.astype(jnp.float32)
    return y.astype(x.dtype)


def _sb_block(q, k, v, q_pos, k_pos):
    z = jnp.einsum('bqhd,bkhd->bhqk', q, k).astype(jnp.float32) * (1.0 / math.sqrt(HEAD_DIM))
    mask = k_pos[None, :] < q_pos[:, None]
    log_rem = jnp.where(mask, jax.nn.log_sigmoid(-z), 0.0)
    after = lax.cumsum(log_rem, axis=3, reverse=True) - log_rem
    w = jnp.where(mask, jnp.exp(jax.nn.log_sigmoid(z) + after), 0.0)
    return jnp.einsum('bhqk,bkhd->bqhd', w.astype(v.dtype), v)


def _sb_prompt(q, k, v):
    B, T, H, Dh = q.shape
    nb = T // Q_BLOCK
    qb = q.reshape(B, nb, Q_BLOCK, H, Dh).transpose(1, 0, 2, 3, 4)
    k_pos = jnp.arange(T)

    def one(args):
        q_blk, i = args
        q_pos = i * Q_BLOCK + jnp.arange(Q_BLOCK)
        return _sb_block(q_blk, k, v, q_pos, k_pos)

    out = lax.map(one, (qb, jnp.arange(nb)))
    return out.transpose(1, 0, 2, 3, 4).reshape(B, T, H * Dh)


def _sb_sample(q, k, v, k_past, v_past):
    B, T, H, Dh = q.shape
    P = k_past.shape[1]
    k_all = jnp.concatenate([k_past.astype(k.dtype), k], axis=1)
    v_all = jnp.concatenate([v_past.astype(v.dtype), v], axis=1)
    q_pos = P + jnp.arange(T)
    k_pos = jnp.arange(P + T)
    return _sb_block(q, k_all, v_all, q_pos, k_pos).reshape(B, T, H * Dh)


def _cmul_combine(e1, e2):
    ar1, ai1, br1, bi1 = e1
    ar2, ai2, br2, bi2 = e2
    ar = ar2 * ar1 - ai2 * ai1
    ai = ar2 * ai1 + ai2 * ar1
    br = ar2 * br1 - ai2 * bi1 + br2
    bi = ar2 * bi1 + ai2 * br1 + bi2
    return (ar, ai, br, bi)


def _ssm_branch(u, h0_re, h0_im, a_re, a_im, log_dt, b_re, b_im, c_re, c_im, d_skip, w_glu, b_glu):
    B, T, _ = u.shape
    f32 = jnp.float32
    a_re = a_re.astype(f32)
    a_im = a_im.astype(f32)
    dt = jnp.exp(log_dt.astype(f32))[:, None]
    mag = jnp.exp(a_re * dt)
    ang = a_im * dt
    ab_re = mag * jnp.cos(ang)
    ab_im = mag * jnp.sin(ang)
    den = a_re * a_re + a_im * a_im
    f_re = ((ab_re - 1.0) * a_re + ab_im * a_im) / den
    f_im = (ab_im * a_re - (ab_re - 1.0) * a_im) / den
    b_re = b_re.astype(f32)
    b_im = b_im.astype(f32)
    bb_re = f_re[..., None] * b_re - f_im[..., None] * b_im
    bb_im = f_re[..., None] * b_im + f_im[..., None] * b_re
    uf = u.astype(f32)
    ug = uf.reshape(B, T, N_SSM_GROUPS, SSM_GROUP)
    bu_re = jnp.einsum('btgc,gpc->tbgp', ug, bb_re)
    bu_im = jnp.einsum('btgc,gpc->tbgp', ug, bb_im)
    if h0_re is not None:
        h0r = h0_re.astype(f32)
        h0i = h0_im.astype(f32)
        bu_re = bu_re.at[0].add(ab_re * h0r - ab_im * h0i)
        bu_im = bu_im.at[0].add(ab_re * h0i + ab_im * h0r)
    a_s_re = jnp.broadcast_to(ab_re, (T, 1, N_SSM_GROUPS, STATE_DIM))
    a_s_im = jnp.broadcast_to(ab_im, (T, 1, N_SSM_GROUPS, STATE_DIM))
    _, _, h_re, h_im = lax.associative_scan(_cmul_combine, (a_s_re, a_s_im, bu_re, bu_im), axis=0)
    y = (jnp.einsum('tbgp,gcp->btgc', h_re, c_re.astype(f32))
         - jnp.einsum('tbgp,gcp->btgc', h_im, c_im.astype(f32)))
    y = y.reshape(B, T, D_SSM) + d_skip.astype(f32) * uf
    y = jax.nn.gelu(y)
    y = y * jax.nn.sigmoid(y @ w_glu.astype(f32) + b_glu.astype(f32))
    return y.astype(u.dtype), h_re[-1], h_im[-1]


def _layer(x, k_past, v_past, h0_re, h0_im, ln_g, w_in, a_re, a_im, log_dt, b_re, b_im,
           c_re, c_im, d_skip, w_glu, b_glu, g_att, g_ssm, w_out):
    B, T, _ = x.shape
    hn = _rmsnorm(x, ln_g)
    proj = hn @ w_in
    q, k, v, ga, u, gs = jnp.split(
        proj, [D_ATT, 2 * D_ATT, 3 * D_ATT, 4 * D_ATT, 4 * D_ATT + D_SSM], axis=-1)
    q = q.reshape(B, T, N_HEADS, HEAD_DIM)
    k = k.reshape(B, T, N_HEADS, HEAD_DIM)
    v = v.reshape(B, T, N_HEADS, HEAD_DIM)
    if k_past is None:
        att = _sb_prompt(q, k, v)
    else:
        att = _sb_sample(q, k, v, k_past, v_past)
    ssm_y, h_re, h_im = _ssm_branch(u, h0_re, h0_im, a_re, a_im, log_dt, b_re, b_im,
                                     c_re, c_im, d_skip, w_glu, b_glu)
    att_o = _rmsnorm(att, g_att) * jax.nn.silu(ga)
    ssm_o = _rmsnorm(ssm_y, g_ssm) * jax.nn.silu(gs)
    out = jnp.concatenate([att_o, ssm_o], axis=-1) @ w_out
    return x + out, k, v, h_re, h_im


def setup_inputs(seed: int = 0) -> dict:
    key = jax.random.key(seed)
    ks = jax.random.split(key, 24)
    f32 = jnp.float32
    G, P, C = N_SSM_GROUPS, STATE_DIM, SSM_GROUP
    x_prompt = jax.random.normal(ks[0], (BATCH, SEQ, D_MODEL), f32)
    x_sample = jax.random.normal(ks[1], (DEC_BATCH, DEC_SEQ, D_MODEL), f32)
    cache_k = jax.random.normal(ks[2], (DEPTH, DEC_BATCH, PAST_LEN, N_HEADS, HEAD_DIM), f32)
    cache_v = jax.random.normal(ks[3], (DEPTH, DEC_BATCH, PAST_LEN, N_HEADS, HEAD_DIM), f32)
    state_ssm_re = 0.5 * jax.random.normal(ks[4], (DEPTH, DEC_BATCH, G, P), f32)
    state_ssm_im = 0.5 * jax.random.normal(ks[5], (DEPTH, DEC_BATCH, G, P), f32)
    ln_g = 1.0 + 0.02 * jax.random.normal(ks[6], (DEPTH, D_MODEL), f32)
    w_in = jax.random.normal(ks[7], (DEPTH, D_MODEL, D_IN), f32) * D_MODEL ** -0.5
    n = jnp.arange(P, dtype=f32)
    ssm_a_re = -0.5 + 0.01 * jax.random.normal(ks[8], (DEPTH, G, P), f32)
    ssm_a_im = math.pi * n + 0.01 * jax.random.normal(ks[9], (DEPTH, G, P), f32)
    ssm_log_dt = jax.random.uniform(ks[10], (DEPTH, G), f32, math.log(1e-3), math.log(1e-1))
    ssm_b_re = jax.random.normal(ks[11], (DEPTH, G, P, C), f32) * (2.0 * C) ** -0.5
    ssm_b_im = jax.random.normal(ks[12], (DEPTH, G, P, C), f32) * (2.0 * C) ** -0.5
    ssm_c_re = jax.random.normal(ks[13], (DEPTH, G, C, P), f32) * (2.0 * P) ** -0.5
    ssm_c_im = jax.random.normal(ks[14], (DEPTH, G, C, P), f32) * (2.0 * P) ** -0.5
    ssm_d = jax.random.normal(ks[15], (DEPTH, D_SSM), f32)
    w_glu = jax.random.normal(ks[16], (DEPTH, D_SSM, D_SSM), f32) * D_SSM ** -0.5
    b_glu = 0.02 * jax.random.normal(ks[17], (DEPTH, D_SSM), f32)
    g_att = 1.0 + 0.02 * jax.random.normal(ks[18], (DEPTH, D_ATT), f32)
    g_ssm = 1.0 + 0.02 * jax.random.normal(ks[19], (DEPTH, D_SSM), f32)
    w_out = jax.random.normal(ks[20], (DEPTH, D_MIX, D_MODEL), f32) * (0.5 * D_MIX ** -0.5)
    final_g = 1.0 + 0.02 * jax.random.normal(ks[21], (D_MODEL,), f32)
    return {'x_prompt': x_prompt, 'x_sample': x_sample,
            'cache_k': cache_k, 'cache_v': cache_v,
            'state_ssm_re': state_ssm_re, 'state_ssm_im': state_ssm_im,
            'ln_g': ln_g, 'w_in': w_in,
            'ssm_a_re': ssm_a_re, 'ssm_a_im': ssm_a_im, 'ssm_log_dt': ssm_log_dt,
            'ssm_b_re': ssm_b_re, 'ssm_b_im': ssm_b_im,
            'ssm_c_re': ssm_c_re, 'ssm_c_im': ssm_c_im, 'ssm_d': ssm_d,
            'w_glu': w_glu, 'b_glu': b_glu, 'g_att': g_att, 'g_ssm': g_ssm,
            'w_out': w_out, 'final_g': final_g}


def reference(x_prompt, x_sample, cache_k, cache_v, state_ssm_re, state_ssm_im,
              ln_g, w_in, ssm_a_re, ssm_a_im, ssm_log_dt, ssm_b_re, ssm_b_im,
              ssm_c_re, ssm_c_im, ssm_d, w_glu, b_glu, g_att, g_ssm, w_out, final_g):
    xp = x_prompt
    xs = x_sample
    kp, vp, hrp, hip = [], [], [], []
    ksl, vsl, hrs, his = [], [], [], []
    for l in range(DEPTH):
        w = (ln_g[l], w_in[l], ssm_a_re[l], ssm_a_im[l], ssm_log_dt[l], ssm_b_re[l], ssm_b_im[l],
             ssm_c_re[l], ssm_c_im[l], ssm_d[l], w_glu[l], b_glu[l], g_att[l], g_ssm[l], w_out[l])
        xp, k1, v1, r1, i1 = _layer(xp, None, None, None, None, *w)
        xs, k2, v2, r2, i2 = _layer(xs, cache_k[l], cache_v[l], state_ssm_re[l], state_ssm_im[l], *w)
        kp.append(k1); vp.append(v1); hrp.append(r1); hip.append(i1)
        ksl.append(k2); vsl.append(v2); hrs.append(r2); his.append(i2)
    y_prompt = _rmsnorm(xp, final_g)
    y_sample = _rmsnorm(xs, final_g)
    return (y_prompt, y_sample,
            jnp.stack(kp), jnp.stack(vp), jnp.stack(hrp), jnp.stack(hip),
            jnp.stack(ksl), jnp.stack(vsl), jnp.stack(hrs), jnp.stack(his))
```

```python
import functools
import math

import jax
import jax.numpy as jnp
from jax import lax
from jax.experimental import pallas as pl
from jax.experimental.pallas import tpu as pltpu

F32 = jnp.float32
BF16 = jnp.bfloat16

EPS = 1e-6
HEAD_DIM = 64
SSM_GROUP = 16
LANES = 128
ATT_TK = 256
ATT_TQ = 128
SSM_CHUNK = 64
ROW_TILE = 512
MIB = 1024 * 1024


def _params(sem, vmem_mib):
    return pltpu.CompilerParams(dimension_semantics=sem, vmem_limit_bytes=vmem_mib * MIB)


def _rms(x, g):
    ms = jnp.mean(x * x, axis=-1, keepdims=True)
    return x * lax.rsqrt(ms + EPS) * g


def _inproj_body(x_ref, g_ref, w_ref, *out_refs):
    hn = _rms(x_ref[...], g_ref[...]).astype(BF16)
    width = out_refs[0].shape[-1]
    for n, o_ref in enumerate(out_refs):
        o_ref[...] = jnp.dot(hn, w_ref[:, n * width:(n + 1) * width], preferred_element_type=F32)


def _inproj(x2d, g, w_bf, n_out):
    m, d = x2d.shape
    width = w_bf.shape[1] // n_out
    tm = min(ROW_TILE, m)
    return pl.pallas_call(
        _inproj_body,
        grid=(m // tm,),
        in_specs=[pl.BlockSpec((tm, d), lambda i: (i, 0)),
                  pl.BlockSpec((1, d), lambda i: (0, 0)),
                  pl.BlockSpec(w_bf.shape, lambda i: (0, 0))],
        out_specs=[pl.BlockSpec((tm, width), lambda i: (i, 0))] * n_out,
        out_shape=[jax.ShapeDtypeStruct((m, width), F32)] * n_out,
        compiler_params=_params(("parallel",), 48),
        name="inproj",
    )(x2d, g.reshape(1, d), w_bf)


def _attn_body(q_ref, k_ref, v_ref, o_ref, kb_ref, vb_ref, *, tq, tk, nq, q_off):
    kb_ref[...] = k_ref[0].astype(BF16)
    vb_ref[...] = v_ref[0].astype(BF16)
    lane = lax.broadcasted_iota(jnp.int32, (1, LANES), 1)
    first_head = lane < HEAD_DIM
    strict = (lax.broadcasted_iota(jnp.int32, (tk, tk), 0)
              > lax.broadcasted_iota(jnp.int32, (tk, tk), 1)).astype(BF16)
    strict2 = jnp.concatenate([strict, strict], axis=0)
    row = lax.broadcasted_iota(jnp.int32, (tq, tk), 0)
    col = lax.broadcasted_iota(jnp.int32, (tq, tk), 1)
    scale = 1.0 / math.sqrt(HEAD_DIM)

    def tile(qh, kt, vt, carry, mask):
        z = lax.dot_general(qh, kt, (((1,), (1,)), ((), ())), preferred_element_type=F32)
        soft = jnp.log(1.0 + jnp.exp(-jnp.abs(z)))
        ls = jnp.minimum(z, 0.0) - soft
        lr = ls - z
        if mask is not None:
            lr = jnp.where(mask, lr, 0.0)
        hi = lr.astype(BF16)
        lo = (lr - hi.astype(F32)).astype(BF16)
        after = jnp.dot(jnp.concatenate([hi, lo], axis=1), strict2, preferred_element_type=F32)
        w = jnp.exp(ls + after + carry)
        if mask is not None:
            w = jnp.where(mask, w, 0.0)
        pv = jnp.dot(w.astype(BF16), vt, preferred_element_type=F32)
        return pv, carry + jnp.sum(lr, axis=1, keepdims=True)

    def q_tile(i, _):
        r0 = pl.multiple_of(i * tq, tq)
        q = q_ref[0, pl.ds(r0, tq), :] * scale
        qa = jnp.where(first_head, q, 0.0).astype(BF16)
        qb = jnp.where(first_head, 0.0, q).astype(BF16)
        t_first = q_off + i * tq
        jmax = (t_first + tq - 1) // tk
        k0 = pl.multiple_of(jmax * tk, tk)
        kt = kb_ref[pl.ds(k0, tk), :]
        vt = vb_ref[pl.ds(k0, tk), :]
        mask = (k0 + col) < (t_first + row)
        zero = jnp.zeros((tq, 1), F32)
        pa, ca = tile(qa, kt, vt, zero, mask)
        pb, cb = tile(qb, kt, vt, zero, mask)

        def kstep(s, st):
            pa, ca, pb, cb = st
            k0 = pl.multiple_of((jmax - 1 - s) * tk, tk)
            kt = kb_ref[pl.ds(k0, tk), :]
            vt = vb_ref[pl.ds(k0, tk), :]
            da, ca = tile(qa, kt, vt, ca, None)
            db, cb = tile(qb, kt, vt, cb, None)
            return pa + da, ca, pb + db, cb

        pa, ca, pb, cb = lax.fori_loop(0, jmax, kstep, (pa, ca, pb, cb))
        o_ref[0, pl.ds(r0, tq), :] = jnp.where(first_head, pa, pb)
        return 0

    lax.fori_loop(0, nq, q_tile, 0)


def _attention(q, k_all, v_all, q_off):
    b, t, c = q.shape
    s = k_all.shape[1]
    tq = min(ATT_TQ, t)
    assert t % tq == 0 and ATT_TK % tq == 0 and q_off % tq == 0 and s % ATT_TK == 0
    body = functools.partial(_attn_body, tq=tq, tk=ATT_TK, nq=t // tq, q_off=q_off)
    qspec = pl.BlockSpec((1, t, LANES), lambda bi, hp: (bi, 0, hp))
    kspec = pl.BlockSpec((1, s, LANES), lambda bi, hp: (bi, 0, hp))
    return pl.pallas_call(
        body,
        grid=(b, c // LANES),
        in_specs=[qspec, kspec, kspec],
        out_specs=qspec,
        out_shape=jax.ShapeDtypeStruct((b, t, c), F32),
        scratch_shapes=[pltpu.VMEM((s, LANES), BF16), pltpu.VMEM((s, LANES), BF16)],
        compiler_params=_params(("parallel", "parallel"), 48),
        name="sb_attention",
    )(q, k_all, v_all)


def _ssm_body(u_ref, h0r_ref, h0i_ref, bd_ref, cm_ref, a_ref, d_ref, wg_ref, bg_ref,
              y_ref, hr_ref, hi_ref, h_sc, utb_sc, bu_sc, ytb_sc, *, chunk, batch, ntile):
    half = bd_ref.shape[2] // 2

    @pl.when(pl.program_id(0) == 0)
    def _():
        for j in range(ntile):
            h_sc[j, 0] = h0r_ref[:, j * half:(j + 1) * half]
            h_sc[j, 1] = h0i_ref[:, j * half:(j + 1) * half]

    for bi in range(batch):
        for j in range(ntile):
            utb_sc[j, pl.ds(bi, chunk, stride=batch), :] = u_ref[bi, :, j * LANES:(j + 1) * LANES]

    for j in range(ntile):
        bu_sc[...] = jnp.dot(utb_sc[j].astype(BF16), bd_ref[j], preferred_element_type=F32)
        ar = a_ref[j, 0:1, :]
        ai = a_ref[j, 1:2, :]

        def step(t, h):
            hr, hi = h
            r0 = pl.multiple_of(t * batch, batch)
            nr = ar * hr - ai * hi + bu_sc[pl.ds(r0, batch), :half]
            ni = ar * hi + ai * hr + bu_sc[pl.ds(r0, batch), half:]
            bu_sc[pl.ds(r0, batch), :half] = nr
            bu_sc[pl.ds(r0, batch), half:] = ni
            return nr, ni

        hr, hi = lax.fori_loop(0, chunk, step, (h_sc[j, 0], h_sc[j, 1]), unroll=8)
        h_sc[j, 0] = hr
        h_sc[j, 1] = hi
        hr_ref[:, j * half:(j + 1) * half] = hr
        hi_ref[:, j * half:(j + 1) * half] = hi
        ytb_sc[j] = jnp.dot(bu_sc[...].astype(BF16), cm_ref[j], preferred_element_type=F32)

    ch = jnp.concatenate([ytb_sc[j] for j in range(ntile)], axis=1)
    uu = jnp.concatenate([utb_sc[j] for j in range(ntile)], axis=1)
    y = jax.nn.gelu(ch + d_ref[...] * uu)
    gl = jnp.dot(y.astype(BF16), wg_ref[...], preferred_element_type=F32) + bg_ref[...]
    out = y * jax.nn.sigmoid(gl)
    for j in range(ntile):
        ytb_sc[j] = out[:, j * LANES:(j + 1) * LANES]
    for bi in range(batch):
        for j in range(ntile):
            y_ref[bi, :, j * LANES:(j + 1) * LANES] = ytb_sc[j, pl.ds(bi, chunk, stride=batch), :]


def _ssm(u, h0r, h0i, bd, cm, a, d, wg_bf, bg):
    b, t, c = u.shape
    ntile = c // LANES
    states = h0r.shape[1]
    chunk = min(SSM_CHUNK, t)
    body = functools.partial(_ssm_body, chunk=chunk, batch=b, ntile=ntile)
    const = lambda shape: pl.BlockSpec(shape, lambda i: (0,) * len(shape))
    return pl.pallas_call(
        body,
        grid=(t // chunk,),
        in_specs=[pl.BlockSpec((b, chunk, c), lambda i: (0, i, 0)),
                  const((b, states)), const((b, states)),
                  const(bd.shape), const(cm.shape), const(a.shape),
                  const((1, c)), const(wg_bf.shape), const((1, c))],
        out_specs=[pl.BlockSpec((b, chunk, c), lambda i: (0, i, 0)),
                   const((b, states)), const((b, states))],
        out_shape=[jax.ShapeDtypeStruct((b, t, c), F32),
                   jax.ShapeDtypeStruct((b, states), F32),
                   jax.ShapeDtypeStruct((b, states), F32)],
        scratch_shapes=[pltpu.VMEM((ntile, 2, b, states // ntile), F32),
                        pltpu.VMEM((ntile, chunk * b, LANES), F32),
                        pltpu.VMEM((chunk * b, 2 * states // ntile), F32),
                        pltpu.VMEM((ntile, chunk * b, LANES), F32)],
        compiler_params=_params(("arbitrary",), 48),
        name="s5_branch",
    )(u, h0r, h0i, bd, cm, a, d.reshape(1, c), wg_bf, bg.reshape(1, c))


def _ssm_weights(a_re, a_im, log_dt, b_re, b_im, c_re, c_im, ntile):
    g, p = a_re.shape
    gt = g // ntile
    dt = jnp.exp(log_dt)[:, None]
    mag = jnp.exp(a_re * dt)
    ang = a_im * dt
    ab_re = mag * jnp.cos(ang)
    ab_im = mag * jnp.sin(ang)
    den = a_re * a_re + a_im * a_im
    f_re = ((ab_re - 1.0) * a_re + ab_im * a_im) / den
    f_im = (ab_im * a_re - (ab_re - 1.0) * a_im) / den
    bb_re = f_re[..., None] * b_re - f_im[..., None] * b_im
    bb_im = f_re[..., None] * b_im + f_im[..., None] * b_re
    eye = jnp.eye(gt, dtype=F32)

    def expand(bb):
        bb = bb.reshape(ntile, gt, p, SSM_GROUP)
        return jnp.einsum("jgpc,gh->jgchp", bb, eye).reshape(ntile, gt * SSM_GROUP, gt * p)

    def contract(cc):
        cc = cc.reshape(ntile, gt, SSM_GROUP, p)
        return jnp.einsum("jgcp,gh->jhpgc", cc, eye).reshape(ntile, gt * p, gt * SSM_GROUP)

    bd = jnp.concatenate([expand(bb_re), expand(bb_im)], axis=2).astype(BF16)
    cm = jnp.concatenate([contract(c_re), -contract(c_im)], axis=1).astype(BF16)
    a = jnp.stack([ab_re.reshape(ntile, gt * p), ab_im.reshape(ntile, gt * p)], axis=1)
    return bd, cm, a


def _outproj_body(att_ref, ga_ref, sy_ref, gs_ref, x_ref, gatt_ref, gssm_ref, w_ref, o_ref):
    ga = ga_ref[...]
    gs = gs_ref[...]
    a = (_rms(att_ref[...], gatt_ref[...]) * (ga * jax.nn.sigmoid(ga))).astype(BF16)
    s = (_rms(sy_ref[...], gssm_ref[...]) * (gs * jax.nn.sigmoid(gs))).astype(BF16)
    d_att = a.shape[1]
    out = (jnp.dot(a, w_ref[:d_att, :], preferred_element_type=F32)
           + jnp.dot(s, w_ref[d_att:, :], preferred_element_type=F32))
    o_ref[...] = x_ref[...] + out


def _outproj(att, ga, sy, gs, x2d, g_att, g_ssm, w_bf):
    m, d = x2d.shape
    c = att.shape[1]
    tm = min(ROW_TILE, m)
    half = pl.BlockSpec((tm, c), lambda i: (i, 0))
    return pl.pallas_call(
        _outproj_body,
        grid=(m // tm,),
        in_specs=[half, half, half, half,
                  pl.BlockSpec((tm, d), lambda i: (i, 0)),
                  pl.BlockSpec((1, c), lambda i: (0, 0)),
                  pl.BlockSpec((1, c), lambda i: (0, 0)),
                  pl.BlockSpec(w_bf.shape, lambda i: (0, 0))],
        out_specs=pl.BlockSpec((tm, d), lambda i: (i, 0)),
        out_shape=jax.ShapeDtypeStruct((m, d), F32),
        compiler_params=_params(("parallel",), 48),
        name="outproj",
    )(att, ga, sy, gs, x2d, g_att.reshape(1, c), g_ssm.reshape(1, c), w_bf)


def _final_norm_body(x_ref, g_ref, o_ref):
    o_ref[...] = _rms(x_ref[...], g_ref[...])


def _final_norm(x2d, g):
    m, d = x2d.shape
    tm = min(ROW_TILE, m)
    return pl.pallas_call(
        _final_norm_body,
        grid=(m // tm,),
        in_specs=[pl.BlockSpec((tm, d), lambda i: (i, 0)), pl.BlockSpec((1, d), lambda i: (0, 0))],
        out_specs=pl.BlockSpec((tm, d), lambda i: (i, 0)),
        out_shape=jax.ShapeDtypeStruct((m, d), F32),
        compiler_params=_params(("parallel",), 32),
        name="final_norm",
    )(x2d, g.reshape(1, d))


def _layer(x, k_past, v_past, h0_re, h0_im, ln_g, w_in_bf, ssm_w, d_skip, w_glu_bf, b_glu,
           g_att, g_ssm, w_out_bf):
    b, t, d = x.shape
    x2d = x.reshape(b * t, d)
    q, k, v, ga, u, gs = _inproj(x2d, ln_g, w_in_bf, 6)
    c = q.shape[1]
    q3, k3, v3 = (z.reshape(b, t, c) for z in (q, k, v))
    if k_past is None:
        att = _attention(q3, k3, v3, 0)
    else:
        p = k_past.shape[1]
        pad = (-(p + t)) % ATT_TK
        zeros = jnp.zeros((b, pad, c), F32)
        k_all = jnp.concatenate([k_past.reshape(b, p, c), k3, zeros], axis=1)
        v_all = jnp.concatenate([v_past.reshape(b, p, c), v3, zeros], axis=1)
        att = _attention(q3, k_all, v_all, p)
    bd, cm, a = ssm_w
    states = a.shape[0] * a.shape[2]
    if h0_re is None:
        h0_re = jnp.zeros((b, states), F32)
        h0_im = jnp.zeros((b, states), F32)
    sy, h_re, h_im = _ssm(u.reshape(b, t, c), h0_re.reshape(b, states), h0_im.reshape(b, states),
                          bd, cm, a, d_skip, w_glu_bf, b_glu)
    x_new = _outproj(att.reshape(b * t, c), ga, sy.reshape(b * t, c), gs, x2d, g_att, g_ssm, w_out_bf)
    return x_new.reshape(b, t, d), k3, v3, h_re, h_im


def kernel(x_prompt, x_sample, cache_k, cache_v, state_ssm_re, state_ssm_im, ln_g, w_in, ssm_a_re, ssm_a_im, ssm_log_dt, ssm_b_re, ssm_b_im, ssm_c_re, ssm_c_im, ssm_d, w_glu, b_glu, g_att, g_ssm, w_out, final_g):
    depth = w_in.shape[0]
    n_groups, n_state = ssm_a_re.shape[1:]
    c = ssm_d.shape[1]
    n_heads = c // HEAD_DIM
    xp, xs = x_prompt, x_sample
    kp, vp, hrp, hip, ksl, vsl, hrs, his = ([] for _ in range(8))
    for l in range(depth):
        ssm_w = _ssm_weights(ssm_a_re[l], ssm_a_im[l], ssm_log_dt[l], ssm_b_re[l], ssm_b_im[l],
                             ssm_c_re[l], ssm_c_im[l], c // LANES)
        w = (ln_g[l], w_in[l].astype(BF16), ssm_w, ssm_d[l], w_glu[l].astype(BF16), b_glu[l],
             g_att[l], g_ssm[l], w_out[l].astype(BF16))
        xp, k1, v1, r1, i1 = _layer(xp, None, None, None, None, *w)
        xs, k2, v2, r2, i2 = _layer(xs, cache_k[l], cache_v[l], state_ssm_re[l], state_ssm_im[l], *w)
        for lst, val in zip((kp, vp, hrp, hip, ksl, vsl, hrs, his), (k1, v1, r1, i1, k2, v2, r2, i2)):
            lst.append(val)

    def heads(zs):
        z = jnp.stack(zs)
        return z.reshape(z.shape[:3] + (n_heads, HEAD_DIM))

    def states(zs):
        z = jnp.stack(zs)
        return z.reshape(z.shape[:2] + (n_groups, n_state))

    bp, tp, d = xp.shape
    bs, ts, _ = xs.shape
    y_prompt = _final_norm(xp.reshape(bp * tp, d), final_g).reshape(bp, tp, d)
    y_sample = _final_norm(xs.reshape(bs * ts, d), final_g).reshape(bs, ts, d)
    return (y_prompt, y_sample, heads(kp), heads(vp), states(hrp), states(hip),
            heads(ksl), heads(vsl), states(hrs), states(his))
```

```python
import functools
import math

import jax
import jax.numpy as jnp
from jax import lax
from jax.experimental import pallas as pl
from jax.experimental.pallas import tpu as pltpu

F32 = jnp.float32
BF16 = jnp.bfloat16

EPS = 1e-6
HEAD_DIM = 64
SSM_GROUP = 16
LANES = 128
ATT_TK = 256
ATT_TQ = 256
SSM_CHUNK = 64
ROW_TILE = 512
MIB = 1024 * 1024


def _params(sem, vmem_mib):
    return pltpu.CompilerParams(dimension_semantics=sem, vmem_limit_bytes=vmem_mib * MIB)


def _rms(x, g):
    ms = jnp.mean(x * x, axis=-1, keepdims=True)
    return x * lax.rsqrt(ms + EPS) * g


def _inproj_body(x_ref, g_ref, w_ref, *out_refs):
    hn = _rms(x_ref[...], g_ref[...]).astype(BF16)
    width = out_refs[0].shape[-1]
    for n, o_ref in enumerate(out_refs):
        o_ref[...] = jnp.dot(hn, w_ref[:, n * width:(n + 1) * width], preferred_element_type=F32)


def _inproj(x2d, g, w_bf, n_out):
    m, d = x2d.shape
    width = w_bf.shape[1] // n_out
    tm = min(ROW_TILE, m)
    return pl.pallas_call(
        _inproj_body,
        grid=(m // tm,),
        in_specs=[pl.BlockSpec((tm, d), lambda i: (i, 0)),
                  pl.BlockSpec((1, d), lambda i: (0, 0)),
                  pl.BlockSpec(w_bf.shape, lambda i: (0, 0))],
        out_specs=[pl.BlockSpec((tm, width), lambda i: (i, 0))] * n_out,
        out_shape=[jax.ShapeDtypeStruct((m, width), F32)] * n_out,
        compiler_params=_params(("parallel",), 48),
        name="inproj",
    )(x2d, g.reshape(1, d), w_bf)


def _attn_body(q_ref, k_ref, v_ref, o_ref, kb_ref, vb_ref, acc_ref, z_ref, ls_ref, after_ref,
               rs_ref, *, tq, tk, nq, q_off):
    kb_ref[...] = k_ref[0].astype(BF16)
    vb_ref[...] = v_ref[0].astype(BF16)
    rows = 2 * tq
    first_head = lax.broadcasted_iota(jnp.int32, (1, LANES), 1) < HEAD_DIM
    strict = (lax.broadcasted_iota(jnp.int32, (tk, tk), 0)
              > lax.broadcasted_iota(jnp.int32, (tk, tk), 1)).astype(BF16)
    qrow = lax.broadcasted_iota(jnp.int32, (rows, tk), 0) & (tq - 1)
    col = lax.broadcasted_iota(jnp.int32, (rows, tk), 1)
    scale = math.log2(math.e) / math.sqrt(HEAD_DIM)
    sign_bit = jnp.uint32(0x80000000)

    def logits(qs, j):
        k0 = pl.multiple_of(jnp.maximum(j, 0) * tk, tk)
        return lax.dot_general(qs, kb_ref[pl.ds(k0, tk), :], (((1,), (1,)), ((), ())),
                               preferred_element_type=F32)

    def middle(mask):
        z = z_ref[...]
        neg_abs = lax.bitcast_convert_type(lax.bitcast_convert_type(z, jnp.uint32) | sign_bit, F32)
        ls = jnp.minimum(z, 0.0) - jnp.log2(1.0 + jnp.exp2(neg_abs))
        lr = ls - z
        if mask is not None:
            lr = jnp.where(mask, lr, 0.0)
        ls_ref[...] = ls
        after_ref[...] = jnp.dot(lr.astype(BF16), strict, preferred_element_type=F32)
        rs_ref[...] = jnp.sum(lr, axis=1, keepdims=True)

    def finish(j, carry, mask):
        k0 = pl.multiple_of(j * tk, tk)
        w = jnp.exp2(ls_ref[...] + after_ref[...] + carry)
        if mask is not None:
            w = jnp.where(mask, w, 0.0)
        pv = jnp.dot(w.astype(BF16), vb_ref[pl.ds(k0, tk), :], preferred_element_type=F32)
        return pv, carry + rs_ref[...]

    def q_tile(i, _):
        r0 = pl.multiple_of(i * tq, tq)
        q = q_ref[0, pl.ds(r0, tq), :] * scale
        qs = jnp.concatenate([jnp.where(first_head, q, 0.0), jnp.where(first_head, 0.0, q)],
                             axis=0).astype(BF16)
        t_first = q_off + i * tq
        jmax = (t_first + tq - 1) // tk
        mask = (jmax * tk + col) < (t_first + qrow)
        z_ref[...] = logits(qs, jmax)
        z_next = logits(qs, jmax - 1)
        middle(mask)
        z_ref[...] = z_next
        z_next = logits(qs, jmax - 2)
        pv, carry = finish(jmax, jnp.zeros((rows, 1), F32), mask)
        acc_ref[...] = pv
        middle(None)
        z_ref[...] = z_next

        def kstep(t, carry):
            z_next = logits(qs, jmax - t)
            pv, carry = finish(jmax - t + 2, carry, None)
            acc_ref[...] += pv
            middle(None)
            z_ref[...] = z_next
            return carry

        lax.fori_loop(3, jmax + 3, kstep, carry)
        o_ref[0, pl.ds(r0, tq), :] = jnp.where(first_head, acc_ref[:tq, :], acc_ref[tq:, :])
        return 0

    lax.fori_loop(0, nq, q_tile, 0)


def _attention(q, k_all, v_all, q_off):
    b, t, c = q.shape
    s = k_all.shape[1]
    tq = min(ATT_TQ, t)
    assert t % tq == 0 and ATT_TK % tq == 0 and q_off % tq == 0 and s % ATT_TK == 0
    body = functools.partial(_attn_body, tq=tq, tk=ATT_TK, nq=t // tq, q_off=q_off)
    qspec = pl.BlockSpec((1, t, LANES), lambda bi, hp: (bi, 0, hp))
    kspec = pl.BlockSpec((1, s, LANES), lambda bi, hp: (bi, 0, hp))
    return pl.pallas_call(
        body,
        grid=(b, c // LANES),
        in_specs=[qspec, kspec, kspec],
        out_specs=qspec,
        out_shape=jax.ShapeDtypeStruct((b, t, c), F32),
        scratch_shapes=[pltpu.VMEM((s, LANES), BF16), pltpu.VMEM((s, LANES), BF16),
                        pltpu.VMEM((2 * tq, LANES), F32),
                        pltpu.VMEM((2 * tq, ATT_TK), F32),
                        pltpu.VMEM((2 * tq, ATT_TK), F32),
                        pltpu.VMEM((2 * tq, ATT_TK), F32),
                        pltpu.VMEM((2 * tq, 1), F32)],
        compiler_params=_params(("parallel", "parallel"), 48),
        name="sb_attention",
    )(q, k_all, v_all)


def _ssm_body(u_ref, h0r_ref, h0i_ref, bd_ref, cm_ref, a_ref, d_ref, wg_ref, bg_ref,
              y_ref, hr_ref, hi_ref, h_sc, utb_sc, bu_sc, ytb_sc, *, chunk, batch, ntile):
    half = bd_ref.shape[2] // 2

    @pl.when(pl.program_id(0) == 0)
    def _():
        for j in range(ntile):
            h_sc[j, 0] = h0r_ref[:, j * half:(j + 1) * half]
            h_sc[j, 1] = h0i_ref[:, j * half:(j + 1) * half]

    for bi in range(batch):
        for j in range(ntile):
            utb_sc[j, pl.ds(bi, chunk, stride=batch), :] = u_ref[bi, :, j * LANES:(j + 1) * LANES]

    for j in range(ntile):
        bu_sc[...] = jnp.dot(utb_sc[j].astype(BF16), bd_ref[j], preferred_element_type=F32)
        ar = a_ref[j, 0:1, :]
        ai = a_ref[j, 1:2, :]

        def step(t, h):
            hr, hi = h
            r0 = pl.multiple_of(t * batch, batch)
            nr = ar * hr - ai * hi + bu_sc[pl.ds(r0, batch), :half]
            ni = ar * hi + ai * hr + bu_sc[pl.ds(r0, batch), half:]
            bu_sc[pl.ds(r0, batch), :half] = nr
            bu_sc[pl.ds(r0, batch), half:] = ni
            return nr, ni

        hr, hi = lax.fori_loop(0, chunk, step, (h_sc[j, 0], h_sc[j, 1]), unroll=8)
        h_sc[j, 0] = hr
        h_sc[j, 1] = hi
        hr_ref[:, j * half:(j + 1) * half] = hr
        hi_ref[:, j * half:(j + 1) * half] = hi
        ytb_sc[j] = jnp.dot(bu_sc[...].astype(BF16), cm_ref[j], preferred_element_type=F32)

    ch = jnp.concatenate([ytb_sc[j] for j in range(ntile)], axis=1)
    uu = jnp.concatenate([utb_sc[j] for j in range(ntile)], axis=1)
    y = jax.nn.gelu(ch + d_ref[...] * uu)
    gl = jnp.dot(y.astype(BF16), wg_ref[...], preferred_element_type=F32) + bg_ref[...]
    out = y * jax.nn.sigmoid(gl)
    for j in range(ntile):
        ytb_sc[j] = out[:, j * LANES:(j + 1) * LANES]
    for bi in range(batch):
        for j in range(ntile):
            y_ref[bi, :, j * LANES:(j + 1) * LANES] = ytb_sc[j, pl.ds(bi, chunk, stride=batch), :]


def _ssm(u, h0r, h0i, bd, cm, a, d, wg_bf, bg):
    b, t, c = u.shape
    ntile = c // LANES
    states = h0r.shape[1]
    chunk = min(SSM_CHUNK, t)
    body = functools.partial(_ssm_body, chunk=chunk, batch=b, ntile=ntile)
    const = lambda shape: pl.BlockSpec(shape, lambda i: (0,) * len(shape))
    return pl.pallas_call(
        body,
        grid=(t // chunk,),
        in_specs=[pl.BlockSpec((b, chunk, c), lambda i: (0, i, 0)),
                  const((b, states)), const((b, states)),
                  const(bd.shape), const(cm.shape), const(a.shape),
                  const((1, c)), const(wg_bf.shape), const((1, c))],
        out_specs=[pl.BlockSpec((b, chunk, c), lambda i: (0, i, 0)),
                   const((b, states)), const((b, states))],
        out_shape=[jax.ShapeDtypeStruct((b, t, c), F32),
                   jax.ShapeDtypeStruct((b, states), F32),
                   jax.ShapeDtypeStruct((b, states), F32)],
        scratch_shapes=[pltpu.VMEM((ntile, 2, b, states // ntile), F32),
                        pltpu.VMEM((ntile, chunk * b, LANES), F32),
                        pltpu.VMEM((chunk * b, 2 * states // ntile), F32),
                        pltpu.VMEM((ntile, chunk * b, LANES), F32)],
        compiler_params=_params(("arbitrary",), 48),
        name="s5_branch",
    )(u, h0r, h0i, bd, cm, a, d.reshape(1, c), wg_bf, bg.reshape(1, c))


def _ssm_weights(a_re, a_im, log_dt, b_re, b_im, c_re, c_im, ntile):
    g, p = a_re.shape
    gt = g // ntile
    dt = jnp.exp(log_dt)[:, None]
    mag = jnp.exp(a_re * dt)
    ang = a_im * dt
    ab_re = mag * jnp.cos(ang)
    ab_im = mag * jnp.sin(ang)
    den = a_re * a_re + a_im * a_im
    f_re = ((ab_re - 1.0) * a_re + ab_im * a_im) / den
    f_im = (ab_im * a_re - (ab_re - 1.0) * a_im) / den
    bb_re = f_re[..., None] * b_re - f_im[..., None] * b_im
    bb_im = f_re[..., None] * b_im + f_im[..., None] * b_re
    eye = jnp.eye(gt, dtype=F32)

    def expand(bb):
        bb = bb.reshape(ntile, gt, p, SSM_GROUP)
        return jnp.einsum("jgpc,gh->jgchp", bb, eye).reshape(ntile, gt * SSM_GROUP, gt * p)

    def contract(cc):
        cc = cc.reshape(ntile, gt, SSM_GROUP, p)
        return jnp.einsum("jgcp,gh->jhpgc", cc, eye).reshape(ntile, gt * p, gt * SSM_GROUP)

    bd = jnp.concatenate([expand(bb_re), expand(bb_im)], axis=2).astype(BF16)
    cm = jnp.concatenate([contract(c_re), -contract(c_im)], axis=1).astype(BF16)
    a = jnp.stack([ab_re.reshape(ntile, gt * p), ab_im.reshape(ntile, gt * p)], axis=1)
    return bd, cm, a


def _outproj_body(att_ref, ga_ref, sy_ref, gs_ref, x_ref, gatt_ref, gssm_ref, w_ref, o_ref):
    ga = ga_ref[...]
    gs = gs_ref[...]
    a = (_rms(att_ref[...], gatt_ref[...]) * (ga * jax.nn.sigmoid(ga))).astype(BF16)
    s = (_rms(sy_ref[...], gssm_ref[...]) * (gs * jax.nn.sigmoid(gs))).astype(BF16)
    d_att = a.shape[1]
    out = (jnp.dot(a, w_ref[:d_att, :], preferred_element_type=F32)
           + jnp.dot(s, w_ref[d_att:, :], preferred_element_type=F32))
    o_ref[...] = x_ref[...] + out


def _outproj(att, ga, sy, gs, x2d, g_att, g_ssm, w_bf):
    m, d = x2d.shape
    c = att.shape[1]
    tm = min(ROW_TILE, m)
    half = pl.BlockSpec((tm, c), lambda i: (i, 0))
    return pl.pallas_call(
        _outproj_body,
        grid=(m // tm,),
        in_specs=[half, half, half, half,
                  pl.BlockSpec((tm, d), lambda i: (i, 0)),
                  pl.BlockSpec((1, c), lambda i: (0, 0)),
                  pl.BlockSpec((1, c), lambda i: (0, 0)),
                  pl.BlockSpec(w_bf.shape, lambda i: (0, 0))],
        out_specs=pl.BlockSpec((tm, d), lambda i: (i, 0)),
        out_shape=jax.ShapeDtypeStruct((m, d), F32),
        compiler_params=_params(("parallel",), 48),
        name="outproj",
    )(att, ga, sy, gs, x2d, g_att.reshape(1, c), g_ssm.reshape(1, c), w_bf)


def _final_norm_body(x_ref, g_ref, o_ref):
    o_ref[...] = _rms(x_ref[...], g_ref[...])


def _final_norm(x2d, g):
    m, d = x2d.shape
    tm = min(ROW_TILE, m)
    return pl.pallas_call(
        _final_norm_body,
        grid=(m // tm,),
        in_specs=[pl.BlockSpec((tm, d), lambda i: (i, 0)), pl.BlockSpec((1, d), lambda i: (0, 0))],
        out_specs=pl.BlockSpec((tm, d), lambda i: (i, 0)),
        out_shape=jax.ShapeDtypeStruct((m, d), F32),
        compiler_params=_params(("parallel",), 32),
        name="final_norm",
    )(x2d, g.reshape(1, d))


def _layer(x, k_past, v_past, h0_re, h0_im, ln_g, w_in_bf, ssm_w, d_skip, w_glu_bf, b_glu,
           g_att, g_ssm, w_out_bf):
    b, t, d = x.shape
    x2d = x.reshape(b * t, d)
    q, k, v, ga, u, gs = _inproj(x2d, ln_g, w_in_bf, 6)
    c = q.shape[1]
    q3, k3, v3 = (z.reshape(b, t, c) for z in (q, k, v))
    if k_past is None:
        att = _attention(q3, k3, v3, 0)
    else:
        p = k_past.shape[1]
        pad = (-(p + t)) % ATT_TK
        zeros = jnp.zeros((b, pad, c), F32)
        k_all = jnp.concatenate([k_past.reshape(b, p, c), k3, zeros], axis=1)
        v_all = jnp.concatenate([v_past.reshape(b, p, c), v3, zeros], axis=1)
        att = _attention(q3, k_all, v_all, p)
    bd, cm, a = ssm_w
    states = a.shape[0] * a.shape[2]
    if h0_re is None:
        h0_re = jnp.zeros((b, states), F32)
        h0_im = jnp.zeros((b, states), F32)
    sy, h_re, h_im = _ssm(u.reshape(b, t, c), h0_re.reshape(b, states), h0_im.reshape(b, states),
                          bd, cm, a, d_skip, w_glu_bf, b_glu)
    x_new = _outproj(att.reshape(b * t, c), ga, sy.reshape(b * t, c), gs, x2d, g_att, g_ssm, w_out_bf)
    return x_new.reshape(b, t, d), k3, v3, h_re, h_im


def kernel(x_prompt, x_sample, cache_k, cache_v, state_ssm_re, state_ssm_im, ln_g, w_in, ssm_a_re, ssm_a_im, ssm_log_dt, ssm_b_re, ssm_b_im, ssm_c_re, ssm_c_im, ssm_d, w_glu, b_glu, g_att, g_ssm, w_out, final_g):
    depth = w_in.shape[0]
    n_groups, n_state = ssm_a_re.shape[1:]
    c = ssm_d.shape[1]
    n_heads = c // HEAD_DIM
    xp, xs = x_prompt, x_sample
    kp, vp, hrp, hip, ksl, vsl, hrs, his = ([] for _ in range(8))
    for l in range(depth):
        ssm_w = _ssm_weights(ssm_a_re[l], ssm_a_im[l], ssm_log_dt[l], ssm_b_re[l], ssm_b_im[l],
                             ssm_c_re[l], ssm_c_im[l], c // LANES)
        w = (ln_g[l], w_in[l].astype(BF16), ssm_w, ssm_d[l], w_glu[l].astype(BF16), b_glu[l],
             g_att[l], g_ssm[l], w_out[l].astype(BF16))
        xp, k1, v1, r1, i1 = _layer(xp, None, None, None, None, *w)
        xs, k2, v2, r2, i2 = _layer(xs, cache_k[l], cache_v[l], state_ssm_re[l], state_ssm_im[l], *w)
        for lst, val in zip((kp, vp, hrp, hip, ksl, vsl, hrs, his), (k1, v1, r1, i1, k2, v2, r2, i2)):
            lst.append(val)

    def heads(zs):
        z = jnp.stack(zs)
        return z.reshape(z.shape[:3] + (n_heads, HEAD_DIM))

    def states(zs):
        z = jnp.stack(zs)
        return z.reshape(z.shape[:2] + (n_groups, n_state))

    bp, tp, d = xp.shape
    bs, ts, _ = xs.shape
    y_prompt = _final_norm(xp.reshape(bp * tp, d), final_g).reshape(bp, tp, d)
    y_sample = _final_norm(xs.reshape(bs * ts, d), final_g).reshape(bs, ts, d)
    return (y_prompt, y_sample, heads(kp), heads(vp), states(hrp), states(hip),
            heads(ksl), heads(vsl), states(hrs), states(his))
```

```python
import functools
import math

import jax
import jax.numpy as jnp
from jax import lax
from jax.experimental import pallas as pl
from jax.experimental.pallas import tpu as pltpu

F32 = jnp.float32
BF16 = jnp.bfloat16

EPS = 1e-6
HEAD_DIM = 64
SSM_GROUP = 16
LANES = 128
ATT_TK = 256
ATT_TQ = 256
ATT_GROUP = 2
DEAD_LOG2 = -160.0
MINUS_INF_LOG2 = -1e30
SSM_CHUNK = 64
ROW_TILE = 512
MIB = 1024 * 1024


def _params(sem, vmem_mib):
    return pltpu.CompilerParams(dimension_semantics=sem, vmem_limit_bytes=vmem_mib * MIB)


def _rms(x, g):
    ms = jnp.mean(x * x, axis=-1, keepdims=True)
    return x * lax.rsqrt(ms + EPS) * g


def _inproj_body(x_ref, g_ref, w_ref, *out_refs):
    hn = _rms(x_ref[...], g_ref[...]).astype(BF16)
    width = out_refs[0].shape[-1]
    for n, o_ref in enumerate(out_refs):
        o_ref[...] = jnp.dot(hn, w_ref[:, n * width:(n + 1) * width], preferred_element_type=F32)


def _inproj(x2d, g, w_bf, n_out):
    m, d = x2d.shape
    width = w_bf.shape[1] // n_out
    tm = min(ROW_TILE, m)
    return pl.pallas_call(
        _inproj_body,
        grid=(m // tm,),
        in_specs=[pl.BlockSpec((tm, d), lambda i: (i, 0)),
                  pl.BlockSpec((1, d), lambda i: (0, 0)),
                  pl.BlockSpec(w_bf.shape, lambda i: (0, 0))],
        out_specs=[pl.BlockSpec((tm, width), lambda i: (i, 0))] * n_out,
        out_shape=[jax.ShapeDtypeStruct((m, width), F32)] * n_out,
        compiler_params=_params(("parallel",), 48),
        name="inproj",
    )(x2d, g.reshape(1, d), w_bf)


def _attn_body(q_ref, k_ref, v_ref, o_ref, kb_ref, vb_ref, acc_ref, z_ref, ls_ref, after_ref,
               rs_ref, *, tq, tk, nq, q_off, group):
    kb_ref[...] = k_ref[0].astype(BF16)
    vb_ref[...] = v_ref[0].astype(BF16)
    rows = 2 * tq
    first_head = lax.broadcasted_iota(jnp.int32, (1, LANES), 1) < HEAD_DIM
    strict = (lax.broadcasted_iota(jnp.int32, (tk, tk), 0)
              > lax.broadcasted_iota(jnp.int32, (tk, tk), 1)).astype(BF16)
    qrow = lax.broadcasted_iota(jnp.int32, (rows, tk), 0) & (tq - 1)
    col = lax.broadcasted_iota(jnp.int32, (rows, tk), 1)
    scale = math.log2(math.e) / math.sqrt(HEAD_DIM)
    sign_bit = jnp.uint32(0x80000000)

    def key_rows(j):
        return pl.ds(pl.multiple_of(jnp.maximum(j, 0) * tk, tk), tk)

    def logits(qs, j):
        return lax.dot_general(qs, kb_ref[key_rows(j), :], (((1,), (1,)), ((), ())),
                               preferred_element_type=F32)

    def log_gates(z, mask):
        if mask is not None:
            z = jnp.where(mask, z, MINUS_INF_LOG2)
        neg_abs = lax.bitcast_convert_type(lax.bitcast_convert_type(z, jnp.uint32) | sign_bit, F32)
        ls = jnp.minimum(z, 0.0) - jnp.log2(1.0 + jnp.exp2(neg_abs))
        lr = ls - z
        return ls, lr

    def suffix(lr):
        return jnp.dot(lr.astype(BF16), strict, preferred_element_type=F32)

    def weighted(w, j):
        return jnp.dot(w.astype(BF16), vb_ref[key_rows(j), :], preferred_element_type=F32)

    def middle():
        ls, lr = log_gates(z_ref[...], None)
        ls_ref[...] = ls
        after_ref[...] = suffix(lr)
        rs_ref[...] = jnp.sum(lr, axis=1, keepdims=True)

    def finish(j, carry):
        pv = weighted(jnp.exp2(ls_ref[...] + after_ref[...] + carry), j)
        return pv, carry + rs_ref[...]

    def q_group(g, _):
        tiles = []
        for n in range(group):
            i = g * group + n
            r0 = pl.multiple_of(i * tq, tq)
            q = q_ref[0, pl.ds(r0, tq), :] * scale
            qs = jnp.concatenate([jnp.where(first_head, q, 0.0), jnp.where(first_head, 0.0, q)],
                                 axis=0).astype(BF16)
            t_first = q_off + i * tq
            jmax = (t_first + tq - 1) // tk
            mask = (jmax * tk + col) < (t_first + qrow)
            tiles.append(dict(r0=r0, qs=qs, jmax=jmax, mask=mask,
                              z0=logits(qs, jmax), z1=logits(qs, jmax - 1)))
        for tl in tiles:
            tl["ls0"], lr0 = log_gates(tl.pop("z0"), tl["mask"])
            tl["after0"] = suffix(lr0)
            tl["rs0"] = jnp.sum(lr0, axis=1, keepdims=True)
            tl["ls1"], lr1 = log_gates(tl.pop("z1"), None)
            tl["after1"] = suffix(lr1)
            tl["rs1"] = jnp.sum(lr1, axis=1, keepdims=True)
        for n, tl in enumerate(tiles):
            jmax = tl["jmax"]
            w0 = jnp.exp2(tl["ls0"] + tl["after0"])
            no_tile1 = jnp.where(jmax >= 1, 0.0, MINUS_INF_LOG2)
            w1 = jnp.exp2(tl["ls1"] + tl["after1"] + (tl["rs0"] + no_tile1))
            acc_ref[n] = weighted(w0, jmax) + weighted(w1, jmax - 1)
            tl["carry"] = tl["rs0"] + tl["rs1"]

        for n, tl in enumerate(tiles):
            qs, jmax, carry = tl["qs"], tl["jmax"], tl["carry"]

            @pl.when(jnp.logical_and(jmax >= 2, jnp.max(carry) > DEAD_LOG2))
            def _(n=n, qs=qs, jmax=jmax, carry=carry):
                z_ref[...] = logits(qs, jmax - 2)
                z_next = logits(qs, jmax - 3)
                middle()
                z_ref[...] = z_next

                def kstep(st):
                    t, carry, _ = st
                    z_next = logits(qs, jmax - t)
                    pv, carry = finish(jmax - t + 2, carry)
                    acc_ref[n] += pv
                    middle()
                    z_ref[...] = z_next
                    return t + 1, carry, jnp.max(carry) > DEAD_LOG2

                lax.while_loop(lambda st: jnp.logical_and(st[0] <= jmax + 2, st[2]), kstep,
                               (jnp.int32(4), carry, jnp.bool_(True)))

            o_ref[0, pl.ds(tl["r0"], tq), :] = jnp.where(first_head, acc_ref[n, :tq, :],
                                                         acc_ref[n, tq:, :])
        return 0

    lax.fori_loop(0, nq // group, q_group, 0)


def _attention(q, k_all, v_all, q_off):
    b, t, c = q.shape
    s = k_all.shape[1]
    tq = min(ATT_TQ, t)
    assert t % tq == 0 and ATT_TK % tq == 0 and q_off % tq == 0 and s % ATT_TK == 0
    nq = t // tq
    group = ATT_GROUP if nq % ATT_GROUP == 0 else 1
    body = functools.partial(_attn_body, tq=tq, tk=ATT_TK, nq=nq, q_off=q_off, group=group)
    qspec = pl.BlockSpec((1, t, LANES), lambda bi, hp: (bi, 0, hp))
    kspec = pl.BlockSpec((1, s, LANES), lambda bi, hp: (bi, 0, hp))
    return pl.pallas_call(
        body,
        grid=(b, c // LANES),
        in_specs=[qspec, kspec, kspec],
        out_specs=qspec,
        out_shape=jax.ShapeDtypeStruct((b, t, c), F32),
        scratch_shapes=[pltpu.VMEM((s, LANES), BF16), pltpu.VMEM((s, LANES), BF16),
                        pltpu.VMEM((group, 2 * tq, LANES), F32),
                        pltpu.VMEM((2 * tq, ATT_TK), F32),
                        pltpu.VMEM((2 * tq, ATT_TK), F32),
                        pltpu.VMEM((2 * tq, ATT_TK), F32),
                        pltpu.VMEM((2 * tq, 1), F32)],
        compiler_params=_params(("parallel", "parallel"), 48),
        name="sb_attention",
    )(q, k_all, v_all)


def _ssm_body(u_ref, h0r_ref, h0i_ref, bd_ref, cm_ref, a_ref, d_ref, wg_ref, bg_ref,
              y_ref, hr_ref, hi_ref, h_sc, utb_sc, bu_sc, ytb_sc, *, chunk, batch, ntile):
    half = bd_ref.shape[2] // 2

    @pl.when(pl.program_id(0) == 0)
    def _():
        for j in range(ntile):
            h_sc[j, 0] = h0r_ref[:, j * half:(j + 1) * half]
            h_sc[j, 1] = h0i_ref[:, j * half:(j + 1) * half]

    for bi in range(batch):
        for j in range(ntile):
            utb_sc[j, pl.ds(bi, chunk, stride=batch), :] = u_ref[bi, :, j * LANES:(j + 1) * LANES]

    for j in range(ntile):
        bu_sc[...] = jnp.dot(utb_sc[j].astype(BF16), bd_ref[j], preferred_element_type=F32)
        ar = a_ref[j, 0:1, :]
        ai = a_ref[j, 1:2, :]

        def step(t, h):
            hr, hi = h
            r0 = pl.multiple_of(t * batch, batch)
            nr = ar * hr - ai * hi + bu_sc[pl.ds(r0, batch), :half]
            ni = ar * hi + ai * hr + bu_sc[pl.ds(r0, batch), half:]
            bu_sc[pl.ds(r0, batch), :half] = nr
            bu_sc[pl.ds(r0, batch), half:] = ni
            return nr, ni

        hr, hi = lax.fori_loop(0, chunk, step, (h_sc[j, 0], h_sc[j, 1]), unroll=8)
        h_sc[j, 0] = hr
        h_sc[j, 1] = hi
        hr_ref[:, j * half:(j + 1) * half] = hr
        hi_ref[:, j * half:(j + 1) * half] = hi
        ytb_sc[j] = jnp.dot(bu_sc[...].astype(BF16), cm_ref[j], preferred_element_type=F32)

    ch = jnp.concatenate([ytb_sc[j] for j in range(ntile)], axis=1)
    uu = jnp.concatenate([utb_sc[j] for j in range(ntile)], axis=1)
    y = jax.nn.gelu(ch + d_ref[...] * uu)
    gl = jnp.dot(y.astype(BF16), wg_ref[...], preferred_element_type=F32) + bg_ref[...]
    out = y * jax.nn.sigmoid(gl)
    for j in range(ntile):
        ytb_sc[j] = out[:, j * LANES:(j + 1) * LANES]
    for bi in range(batch):
        for j in range(ntile):
            y_ref[bi, :, j * LANES:(j + 1) * LANES] = ytb_sc[j, pl.ds(bi, chunk, stride=batch), :]


def _ssm(u, h0r, h0i, bd, cm, a, d, wg_bf, bg):
    b, t, c = u.shape
    ntile = c // LANES
    states = h0r.shape[1]
    chunk = min(SSM_CHUNK, t)
    body = functools.partial(_ssm_body, chunk=chunk, batch=b, ntile=ntile)
    const = lambda shape: pl.BlockSpec(shape, lambda i: (0,) * len(shape))
    return pl.pallas_call(
        body,
        grid=(t // chunk,),
        in_specs=[pl.BlockSpec((b, chunk, c), lambda i: (0, i, 0)),
                  const((b, states)), const((b, states)),
                  const(bd.shape), const(cm.shape), const(a.shape),
                  const((1, c)), const(wg_bf.shape), const((1, c))],
        out_specs=[pl.BlockSpec((b, chunk, c), lambda i: (0, i, 0)),
                   const((b, states)), const((b, states))],
        out_shape=[jax.ShapeDtypeStruct((b, t, c), F32),
                   jax.ShapeDtypeStruct((b, states), F32),
                   jax.ShapeDtypeStruct((b, states), F32)],
        scratch_shapes=[pltpu.VMEM((ntile, 2, b, states // ntile), F32),
                        pltpu.VMEM((ntile, chunk * b, LANES), F32),
                        pltpu.VMEM((chunk * b, 2 * states // ntile), F32),
                        pltpu.VMEM((ntile, chunk * b, LANES), F32)],
        compiler_params=_params(("arbitrary",), 48),
        name="s5_branch",
    )(u, h0r, h0i, bd, cm, a, d.reshape(1, c), wg_bf, bg.reshape(1, c))


def _ssm_weights(a_re, a_im, log_dt, b_re, b_im, c_re, c_im, ntile):
    g, p = a_re.shape
    gt = g // ntile
    dt = jnp.exp(log_dt)[:, None]
    mag = jnp.exp(a_re * dt)
    ang = a_im * dt
    ab_re = mag * jnp.cos(ang)
    ab_im = mag * jnp.sin(ang)
    den = a_re * a_re + a_im * a_im
    f_re = ((ab_re - 1.0) * a_re + ab_im * a_im) / den
    f_im = (ab_im * a_re - (ab_re - 1.0) * a_im) / den
    bb_re = f_re[..., None] * b_re - f_im[..., None] * b_im
    bb_im = f_re[..., None] * b_im + f_im[..., None] * b_re
    eye = jnp.eye(gt, dtype=F32)

    def expand(bb):
        bb = bb.reshape(ntile, gt, p, SSM_GROUP)
        return jnp.einsum("jgpc,gh->jgchp", bb, eye).reshape(ntile, gt * SSM_GROUP, gt * p)

    def contract(cc):
        cc = cc.reshape(ntile, gt, SSM_GROUP, p)
        return jnp.einsum("jgcp,gh->jhpgc", cc, eye).reshape(ntile, gt * p, gt * SSM_GROUP)

    bd = jnp.concatenate([expand(bb_re), expand(bb_im)], axis=2).astype(BF16)
    cm = jnp.concatenate([contract(c_re), -contract(c_im)], axis=1).astype(BF16)
    a = jnp.stack([ab_re.reshape(ntile, gt * p), ab_im.reshape(ntile, gt * p)], axis=1)
    return bd, cm, a


def _outproj_body(att_ref, ga_ref, sy_ref, gs_ref, x_ref, gatt_ref, gssm_ref, w_ref, o_ref):
    ga = ga_ref[...]
    gs = gs_ref[...]
    a = (_rms(att_ref[...], gatt_ref[...]) * (ga * jax.nn.sigmoid(ga))).astype(BF16)
    s = (_rms(sy_ref[...], gssm_ref[...]) * (gs * jax.nn.sigmoid(gs))).astype(BF16)
    d_att = a.shape[1]
    out = (jnp.dot(a, w_ref[:d_att, :], preferred_element_type=F32)
           + jnp.dot(s, w_ref[d_att:, :], preferred_element_type=F32))
    o_ref[...] = x_ref[...] + out


def _outproj(att, ga, sy, gs, x2d, g_att, g_ssm, w_bf):
    m, d = x2d.shape
    c = att.shape[1]
    tm = min(ROW_TILE, m)
    half = pl.BlockSpec((tm, c), lambda i: (i, 0))
    return pl.pallas_call(
        _outproj_body,
        grid=(m // tm,),
        in_specs=[half, half, half, half,
                  pl.BlockSpec((tm, d), lambda i: (i, 0)),
                  pl.BlockSpec((1, c), lambda i: (0, 0)),
                  pl.BlockSpec((1, c), lambda i: (0, 0)),
                  pl.BlockSpec(w_bf.shape, lambda i: (0, 0))],
        out_specs=pl.BlockSpec((tm, d), lambda i: (i, 0)),
        out_shape=jax.ShapeDtypeStruct((m, d), F32),
        compiler_params=_params(("parallel",), 48),
        name="outproj",
    )(att, ga, sy, gs, x2d, g_att.reshape(1, c), g_ssm.reshape(1, c), w_bf)


def _final_norm_body(x_ref, g_ref, o_ref):
    o_ref[...] = _rms(x_ref[...], g_ref[...])


def _final_norm(x2d, g):
    m, d = x2d.shape
    tm = min(ROW_TILE, m)
    return pl.pallas_call(
        _final_norm_body,
        grid=(m // tm,),
        in_specs=[pl.BlockSpec((tm, d), lambda i: (i, 0)), pl.BlockSpec((1, d), lambda i: (0, 0))],
        out_specs=pl.BlockSpec((tm, d), lambda i: (i, 0)),
        out_shape=jax.ShapeDtypeStruct((m, d), F32),
        compiler_params=_params(("parallel",), 32),
        name="final_norm",
    )(x2d, g.reshape(1, d))


def _layer(x, k_past, v_past, h0_re, h0_im, ln_g, w_in_bf, ssm_w, d_skip, w_glu_bf, b_glu,
           g_att, g_ssm, w_out_bf):
    b, t, d = x.shape
    x2d = x.reshape(b * t, d)
    q, k, v, ga, u, gs = _inproj(x2d, ln_g, w_in_bf, 6)
    c = q.shape[1]
    q3, k3, v3 = (z.reshape(b, t, c) for z in (q, k, v))
    if k_past is None:
        att = _attention(q3, k3, v3, 0)
    else:
        p = k_past.shape[1]
        pad = (-(p + t)) % ATT_TK
        zeros = jnp.zeros((b, pad, c), F32)
        k_all = jnp.concatenate([k_past.reshape(b, p, c), k3, zeros], axis=1)
        v_all = jnp.concatenate([v_past.reshape(b, p, c), v3, zeros], axis=1)
        att = _attention(q3, k_all, v_all, p)
    bd, cm, a = ssm_w
    states = a.shape[0] * a.shape[2]
    if h0_re is None:
        h0_re = jnp.zeros((b, states), F32)
        h0_im = jnp.zeros((b, states), F32)
    sy, h_re, h_im = _ssm(u.reshape(b, t, c), h0_re.reshape(b, states), h0_im.reshape(b, states),
                          bd, cm, a, d_skip, w_glu_bf, b_glu)
    x_new = _outproj(att.reshape(b * t, c), ga, sy.reshape(b * t, c), gs, x2d, g_att, g_ssm, w_out_bf)
    return x_new.reshape(b, t, d), k3, v3, h_re, h_im


def kernel(x_prompt, x_sample, cache_k, cache_v, state_ssm_re, state_ssm_im, ln_g, w_in, ssm_a_re, ssm_a_im, ssm_log_dt, ssm_b_re, ssm_b_im, ssm_c_re, ssm_c_im, ssm_d, w_glu, b_glu, g_att, g_ssm, w_out, final_g):
    depth = w_in.shape[0]
    n_groups, n_state = ssm_a_re.shape[1:]
    c = ssm_d.shape[1]
    n_heads = c // HEAD_DIM
    xp, xs = x_prompt, x_sample
    kp, vp, hrp, hip, ksl, vsl, hrs, his = ([] for _ in range(8))
    for l in range(depth):
        ssm_w = _ssm_weights(ssm_a_re[l], ssm_a_im[l], ssm_log_dt[l], ssm_b_re[l], ssm_b_im[l],
                             ssm_c_re[l], ssm_c_im[l], c // LANES)
        w = (ln_g[l], w_in[l].astype(BF16), ssm_w, ssm_d[l], w_glu[l].astype(BF16), b_glu[l],
             g_att[l], g_ssm[l], w_out[l].astype(BF16))
        xp, k1, v1, r1, i1 = _layer(xp, None, None, None, None, *w)
        xs, k2, v2, r2, i2 = _layer(xs, cache_k[l], cache_v[l], state_ssm_re[l], state_ssm_im[l], *w)
        for lst, val in zip((kp, vp, hrp, hip, ksl, vsl, hrs, his), (k1, v1, r1, i1, k2, v2, r2, i2)):
            lst.append(val)

    def heads(zs):
        z = jnp.stack(zs)
        return z.reshape(z.shape[:3] + (n_heads, HEAD_DIM))

    def states(zs):
        z = jnp.stack(zs)
        return z.reshape(z.shape[:2] + (n_groups, n_state))

    bp, tp, d = xp.shape
    bs, ts, _ = xs.shape
    y_prompt = _final_norm(xp.reshape(bp * tp, d), final_g).reshape(bp, tp, d)
    y_sample = _final_norm(xs.reshape(bs * ts, d), final_g).reshape(bs, ts, d)
    return (y_prompt, y_sample, heads(kp), heads(vp), states(hrp), states(hip),
            heads(ksl), heads(vsl), states(hrs), states(his))
```

```python
import functools
import math

import jax
import jax.numpy as jnp
from jax import lax
from jax.experimental import pallas as pl
from jax.experimental.pallas import tpu as pltpu

F32 = jnp.float32
BF16 = jnp.bfloat16

EPS = 1e-6
HEAD_DIM = 64
SSM_GROUP = 16
LANES = 128
ATT_TK = 256
ATT_TQ = 256
ATT_GROUP = 2
DEAD_LOG2 = -160.0
MINUS_INF_LOG2 = -1e30
SSM_CHUNK = 64
ROW_TILE = 512
MIB = 1024 * 1024


def _params(sem, vmem_mib):
    return pltpu.CompilerParams(dimension_semantics=sem, vmem_limit_bytes=vmem_mib * MIB)


def _rms(x, g):
    ms = jnp.mean(x * x, axis=-1, keepdims=True)
    return x * lax.rsqrt(ms + EPS) * g


def _inproj_body(x_ref, g_ref, w_ref, *out_refs):
    hn = _rms(x_ref[...], g_ref[...]).astype(BF16)
    width = out_refs[0].shape[-1]
    for n, o_ref in enumerate(out_refs):
        o_ref[...] = jnp.dot(hn, w_ref[:, n * width:(n + 1) * width], preferred_element_type=F32)


def _inproj(x2d, g, w_bf, n_out):
    m, d = x2d.shape
    width = w_bf.shape[1] // n_out
    tm = min(ROW_TILE, m)
    return pl.pallas_call(
        _inproj_body,
        grid=(m // tm,),
        in_specs=[pl.BlockSpec((tm, d), lambda i: (i, 0)),
                  pl.BlockSpec((1, d), lambda i: (0, 0)),
                  pl.BlockSpec(w_bf.shape, lambda i: (0, 0))],
        out_specs=[pl.BlockSpec((tm, width), lambda i: (i, 0))] * n_out,
        out_shape=[jax.ShapeDtypeStruct((m, width), F32)] * n_out,
        compiler_params=_params(("parallel",), 48),
        name="inproj",
    )(x2d, g.reshape(1, d), w_bf)


def _attn_body(q_ref, k_ref, v_ref, o_ref, unfinished_ref, kb_ref, vb_ref, acc_ref, z_ref, ls_ref,
               after_ref, rs_ref, *, tq, tk, nq, q_off, group):
    kb_ref[...] = k_ref[0].astype(BF16)
    vb_ref[...] = v_ref[0].astype(BF16)
    rows = 2 * tq
    first_head = lax.broadcasted_iota(jnp.int32, (1, LANES), 1) < HEAD_DIM
    strict = (lax.broadcasted_iota(jnp.int32, (tk, tk), 0)
              > lax.broadcasted_iota(jnp.int32, (tk, tk), 1)).astype(BF16)
    qrow = lax.broadcasted_iota(jnp.int32, (rows, tk), 0) & (tq - 1)
    col = lax.broadcasted_iota(jnp.int32, (rows, tk), 1)
    scale = math.log2(math.e) / math.sqrt(HEAD_DIM)
    sign_bit = jnp.uint32(0x80000000)

    def key_rows(j):
        return pl.ds(pl.multiple_of(jnp.maximum(j, 0) * tk, tk), tk)

    def logits(qs, j):
        return lax.dot_general(qs, kb_ref[key_rows(j), :], (((1,), (1,)), ((), ())),
                               preferred_element_type=F32)

    def log_gates(z, mask):
        if mask is not None:
            z = jnp.where(mask, z, MINUS_INF_LOG2)
        neg_abs = lax.bitcast_convert_type(lax.bitcast_convert_type(z, jnp.uint32) | sign_bit, F32)
        ls = jnp.minimum(z, 0.0) - jnp.log2(1.0 + jnp.exp2(neg_abs))
        lr = ls - z
        return ls, lr

    def suffix(lr):
        return jnp.dot(lr.astype(BF16), strict, preferred_element_type=F32)

    def weighted(w, j):
        return jnp.dot(w.astype(BF16), vb_ref[key_rows(j), :], preferred_element_type=F32)

    def middle():
        ls, lr = log_gates(z_ref[...], None)
        ls_ref[...] = ls
        after_ref[...] = suffix(lr)
        rs_ref[...] = jnp.sum(lr, axis=1, keepdims=True)

    def finish(j, carry):
        pv = weighted(jnp.exp2(ls_ref[...] + after_ref[...] + carry), j)
        return pv, carry + rs_ref[...]

    unfinished_ref[...] = jnp.zeros(unfinished_ref.shape, F32)

    def mark_unfinished(flag):
        unfinished_ref[...] = jnp.maximum(unfinished_ref[...], jnp.where(flag, 1.0, 0.0))

    def q_group(g, _):
        tiles = []
        for n in range(group):
            i = g * group + n
            r0 = pl.multiple_of(i * tq, tq)
            q = q_ref[0, pl.ds(r0, tq), :] * scale
            qs = jnp.concatenate([jnp.where(first_head, q, 0.0), jnp.where(first_head, 0.0, q)],
                                 axis=0).astype(BF16)
            t_first = q_off + i * tq
            jmax = (t_first + tq - 1) // tk
            mask = (jmax * tk + col) < (t_first + qrow)
            tiles.append(dict(r0=r0, qs=qs, jmax=jmax, mask=mask,
                              z0=logits(qs, jmax), z1=logits(qs, jmax - 1)))
        for tl in tiles:
            tl["ls0"], lr0 = log_gates(tl.pop("z0"), tl["mask"])
            tl["after0"] = suffix(lr0)
            tl["rs0"] = jnp.sum(lr0, axis=1, keepdims=True)
            tl["ls1"], lr1 = log_gates(tl.pop("z1"), None)
            tl["after1"] = suffix(lr1)
            tl["rs1"] = jnp.sum(lr1, axis=1, keepdims=True)
        for n, tl in enumerate(tiles):
            jmax = tl["jmax"]
            w0 = jnp.exp2(tl["ls0"] + tl["after0"])
            no_tile1 = jnp.where(jmax >= 1, 0.0, MINUS_INF_LOG2)
            w1 = jnp.exp2(tl["ls1"] + tl["after1"] + (tl["rs0"] + no_tile1))
            acc_ref[n] = weighted(w0, jmax) + weighted(w1, jmax - 1)
            tl["carry"] = tl["rs0"] + jnp.where(jmax >= 1, tl["rs1"], 0.0)

        for n, tl in enumerate(tiles):
            qs, jmax, carry = tl["qs"], tl["jmax"], tl["carry"]
            alive = jnp.max(carry) > DEAD_LOG2
            mark_unfinished(jnp.logical_and(jmax < 2, alive))

            @pl.when(jnp.logical_and(jmax >= 2, alive))
            def _(n=n, qs=qs, jmax=jmax, carry=carry):
                z_ref[...] = logits(qs, jmax - 2)
                z_next = logits(qs, jmax - 3)
                middle()
                z_ref[...] = z_next

                def kstep(st):
                    t, carry, _ = st
                    z_next = logits(qs, jmax - t)
                    pv, carry = finish(jmax - t + 2, carry)
                    acc_ref[n] += pv
                    middle()
                    z_ref[...] = z_next
                    return t + 1, carry, jnp.max(carry) > DEAD_LOG2

                st = lax.while_loop(lambda st: jnp.logical_and(st[0] <= jmax + 2, st[2]), kstep,
                                    (jnp.int32(4), carry, jnp.bool_(True)))
                mark_unfinished(st[2])

            o_ref[0, pl.ds(tl["r0"], tq), :] = jnp.where(first_head, acc_ref[n, :tq, :],
                                                         acc_ref[n, tq:, :])
        return 0

    lax.fori_loop(0, nq // group, q_group, 0)


def _attention(q, k_all, v_all, q_off):
    b, t, c = q.shape
    s = k_all.shape[1]
    tq = min(ATT_TQ, t)
    assert t % tq == 0 and ATT_TK % tq == 0 and q_off % tq == 0 and s % ATT_TK == 0
    nq = t // tq
    group = ATT_GROUP if nq % ATT_GROUP == 0 else 1
    body = functools.partial(_attn_body, tq=tq, tk=ATT_TK, nq=nq, q_off=q_off, group=group)
    qspec = pl.BlockSpec((1, t, LANES), lambda bi, hp: (bi, 0, hp))
    kspec = pl.BlockSpec((1, s, LANES), lambda bi, hp: (bi, 0, hp))
    return pl.pallas_call(
        body,
        grid=(b, c // LANES),
        in_specs=[qspec, kspec, kspec],
        out_specs=[qspec, pl.BlockSpec((1, 1, 8, LANES), lambda bi, hp: (bi, hp, 0, 0))],
        out_shape=[jax.ShapeDtypeStruct((b, t, c), F32),
                   jax.ShapeDtypeStruct((b, c // LANES, 8, LANES), F32)],
        scratch_shapes=[pltpu.VMEM((s, LANES), BF16), pltpu.VMEM((s, LANES), BF16),
                        pltpu.VMEM((group, 2 * tq, LANES), F32),
                        pltpu.VMEM((2 * tq, ATT_TK), F32),
                        pltpu.VMEM((2 * tq, ATT_TK), F32),
                        pltpu.VMEM((2 * tq, ATT_TK), F32),
                        pltpu.VMEM((2 * tq, 1), F32)],
        compiler_params=_params(("parallel", "parallel"), 48),
        name="sb_attention",
    )(q, k_all, v_all)


def _ssm_body(u_ref, h0r_ref, h0i_ref, bd_ref, cm_ref, a_ref, d_ref, wg_ref, bg_ref,
              y_ref, hr_ref, hi_ref, h_sc, utb_sc, bu_sc, ytb_sc, *, chunk, batch, ntile):
    half = bd_ref.shape[2] // 2

    @pl.when(pl.program_id(0) == 0)
    def _():
        for j in range(ntile):
            h_sc[j, 0] = h0r_ref[:, j * half:(j + 1) * half]
            h_sc[j, 1] = h0i_ref[:, j * half:(j + 1) * half]

    for bi in range(batch):
        for j in range(ntile):
            utb_sc[j, pl.ds(bi, chunk, stride=batch), :] = u_ref[bi, :, j * LANES:(j + 1) * LANES]

    for j in range(ntile):
        bu_sc[...] = jnp.dot(utb_sc[j].astype(BF16), bd_ref[j], preferred_element_type=F32)
        ar = a_ref[j, 0:1, :]
        ai = a_ref[j, 1:2, :]

        def step(t, h):
            hr, hi = h
            r0 = pl.multiple_of(t * batch, batch)
            nr = ar * hr - ai * hi + bu_sc[pl.ds(r0, batch), :half]
            ni = ar * hi + ai * hr + bu_sc[pl.ds(r0, batch), half:]
            bu_sc[pl.ds(r0, batch), :half] = nr
            bu_sc[pl.ds(r0, batch), half:] = ni
            return nr, ni

        hr, hi = lax.fori_loop(0, chunk, step, (h_sc[j, 0], h_sc[j, 1]), unroll=8)
        h_sc[j, 0] = hr
        h_sc[j, 1] = hi
        hr_ref[:, j * half:(j + 1) * half] = hr
        hi_ref[:, j * half:(j + 1) * half] = hi
        ytb_sc[j] = jnp.dot(bu_sc[...].astype(BF16), cm_ref[j], preferred_element_type=F32)

    ch = jnp.concatenate([ytb_sc[j] for j in range(ntile)], axis=1)
    uu = jnp.concatenate([utb_sc[j] for j in range(ntile)], axis=1)
    y = jax.nn.gelu(ch + d_ref[...] * uu)
    gl = jnp.dot(y.astype(BF16), wg_ref[...], preferred_element_type=F32) + bg_ref[...]
    out = y * jax.nn.sigmoid(gl)
    for j in range(ntile):
        ytb_sc[j] = out[:, j * LANES:(j + 1) * LANES]
    for bi in range(batch):
        for j in range(ntile):
            y_ref[bi, :, j * LANES:(j + 1) * LANES] = ytb_sc[j, pl.ds(bi, chunk, stride=batch), :]


def _ssm(u, h0r, h0i, bd, cm, a, d, wg_bf, bg):
    b, t, c = u.shape
    ntile = c // LANES
    states = h0r.shape[1]
    chunk = min(SSM_CHUNK, t)
    body = functools.partial(_ssm_body, chunk=chunk, batch=b, ntile=ntile)
    const = lambda shape: pl.BlockSpec(shape, lambda i: (0,) * len(shape))
    return pl.pallas_call(
        body,
        grid=(t // chunk,),
        in_specs=[pl.BlockSpec((b, chunk, c), lambda i: (0, i, 0)),
                  const((b, states)), const((b, states)),
                  const(bd.shape), const(cm.shape), const(a.shape),
                  const((1, c)), const(wg_bf.shape), const((1, c))],
        out_specs=[pl.BlockSpec((b, chunk, c), lambda i: (0, i, 0)),
                   const((b, states)), const((b, states))],
        out_shape=[jax.ShapeDtypeStruct((b, t, c), F32),
                   jax.ShapeDtypeStruct((b, states), F32),
                   jax.ShapeDtypeStruct((b, states), F32)],
        scratch_shapes=[pltpu.VMEM((ntile, 2, b, states // ntile), F32),
                        pltpu.VMEM((ntile, chunk * b, LANES), F32),
                        pltpu.VMEM((chunk * b, 2 * states // ntile), F32),
                        pltpu.VMEM((ntile, chunk * b, LANES), F32)],
        compiler_params=_params(("arbitrary",), 48),
        name="s5_branch",
    )(u, h0r, h0i, bd, cm, a, d.reshape(1, c), wg_bf, bg.reshape(1, c))


def _ssm_weights(a_re, a_im, log_dt, b_re, b_im, c_re, c_im, ntile):
    g, p = a_re.shape
    gt = g // ntile
    dt = jnp.exp(log_dt)[:, None]
    mag = jnp.exp(a_re * dt)
    ang = a_im * dt
    ab_re = mag * jnp.cos(ang)
    ab_im = mag * jnp.sin(ang)
    den = a_re * a_re + a_im * a_im
    f_re = ((ab_re - 1.0) * a_re + ab_im * a_im) / den
    f_im = (ab_im * a_re - (ab_re - 1.0) * a_im) / den
    bb_re = f_re[..., None] * b_re - f_im[..., None] * b_im
    bb_im = f_re[..., None] * b_im + f_im[..., None] * b_re
    eye = jnp.eye(gt, dtype=F32)

    def expand(bb):
        bb = bb.reshape(ntile, gt, p, SSM_GROUP)
        return jnp.einsum("jgpc,gh->jgchp", bb, eye).reshape(ntile, gt * SSM_GROUP, gt * p)

    def contract(cc):
        cc = cc.reshape(ntile, gt, SSM_GROUP, p)
        return jnp.einsum("jgcp,gh->jhpgc", cc, eye).reshape(ntile, gt * p, gt * SSM_GROUP)

    bd = jnp.concatenate([expand(bb_re), expand(bb_im)], axis=2).astype(BF16)
    cm = jnp.concatenate([contract(c_re), -contract(c_im)], axis=1).astype(BF16)
    a = jnp.stack([ab_re.reshape(ntile, gt * p), ab_im.reshape(ntile, gt * p)], axis=1)
    return bd, cm, a


def _outproj_body(att_ref, ga_ref, sy_ref, gs_ref, x_ref, gatt_ref, gssm_ref, w_ref, gout_ref,
                  o_ref, *, norm_out):
    ga = ga_ref[...]
    gs = gs_ref[...]
    a = (_rms(att_ref[...], gatt_ref[...]) * (ga * jax.nn.sigmoid(ga))).astype(BF16)
    s = (_rms(sy_ref[...], gssm_ref[...]) * (gs * jax.nn.sigmoid(gs))).astype(BF16)
    d_att = a.shape[1]
    out = (jnp.dot(a, w_ref[:d_att, :], preferred_element_type=F32)
           + jnp.dot(s, w_ref[d_att:, :], preferred_element_type=F32))
    x_new = x_ref[...] + out
    o_ref[...] = _rms(x_new, gout_ref[...]) if norm_out else x_new


def _outproj(att, ga, sy, gs, x2d, g_att, g_ssm, w_bf, g_out, norm_out):
    m, d = x2d.shape
    c = att.shape[1]
    tm = min(ROW_TILE, m)
    half = pl.BlockSpec((tm, c), lambda i: (i, 0))
    return pl.pallas_call(
        functools.partial(_outproj_body, norm_out=norm_out),
        grid=(m // tm,),
        in_specs=[half, half, half, half,
                  pl.BlockSpec((tm, d), lambda i: (i, 0)),
                  pl.BlockSpec((1, c), lambda i: (0, 0)),
                  pl.BlockSpec((1, c), lambda i: (0, 0)),
                  pl.BlockSpec(w_bf.shape, lambda i: (0, 0)),
                  pl.BlockSpec((1, d), lambda i: (0, 0))],
        out_specs=pl.BlockSpec((tm, d), lambda i: (i, 0)),
        out_shape=jax.ShapeDtypeStruct((m, d), F32),
        compiler_params=_params(("parallel",), 48),
        name="outproj",
    )(att, ga, sy, gs, x2d, g_att.reshape(1, c), g_ssm.reshape(1, c), w_bf, g_out.reshape(1, d))


def _sample_attention(q3, k3, v3, k_past, v_past):
    b, t, c = q3.shape
    p = k_past.shape[1]

    def run(n_past):
        zeros = jnp.zeros((b, (-(n_past + t)) % ATT_TK, c), F32)
        k_all = jnp.concatenate([k_past[:, p - n_past:].reshape(b, n_past, c), k3, zeros], axis=1)
        v_all = jnp.concatenate([v_past[:, p - n_past:].reshape(b, n_past, c), v3, zeros], axis=1)
        return _attention(q3, k_all, v_all, n_past)

    if p <= ATT_TK:
        return run(p)[0]
    att, unfinished = run(ATT_TK)
    return lax.cond(jnp.max(unfinished) > 0.0, lambda: run(p)[0], lambda: att)


def _layer(x, k_past, v_past, h0_re, h0_im, ln_g, w_in_bf, ssm_w, d_skip, w_glu_bf, b_glu,
           g_att, g_ssm, w_out_bf, g_out, norm_out):
    b, t, d = x.shape
    x2d = x.reshape(b * t, d)
    q, k, v, ga, u, gs = _inproj(x2d, ln_g, w_in_bf, 6)
    c = q.shape[1]
    q3, k3, v3 = (z.reshape(b, t, c) for z in (q, k, v))
    if k_past is None:
        att, _ = _attention(q3, k3, v3, 0)
    else:
        att = _sample_attention(q3, k3, v3, k_past, v_past)
    bd, cm, a = ssm_w
    states = a.shape[0] * a.shape[2]
    if h0_re is None:
        h0_re = jnp.zeros((b, states), F32)
        h0_im = jnp.zeros((b, states), F32)
    sy, h_re, h_im = _ssm(u.reshape(b, t, c), h0_re.reshape(b, states), h0_im.reshape(b, states),
                          bd, cm, a, d_skip, w_glu_bf, b_glu)
    x_new = _outproj(att.reshape(b * t, c), ga, sy.reshape(b * t, c), gs, x2d, g_att, g_ssm, w_out_bf,
                     g_out, norm_out)
    return x_new.reshape(b, t, d), k3, v3, h_re, h_im


def kernel(x_prompt, x_sample, cache_k, cache_v, state_ssm_re, state_ssm_im, ln_g, w_in, ssm_a_re, ssm_a_im, ssm_log_dt, ssm_b_re, ssm_b_im, ssm_c_re, ssm_c_im, ssm_d, w_glu, b_glu, g_att, g_ssm, w_out, final_g):
    depth = w_in.shape[0]
    n_groups, n_state = ssm_a_re.shape[1:]
    c = ssm_d.shape[1]
    n_heads = c // HEAD_DIM
    xp, xs = x_prompt, x_sample
    kp, vp, hrp, hip, ksl, vsl, hrs, his = ([] for _ in range(8))
    for l in range(depth):
        ssm_w = _ssm_weights(ssm_a_re[l], ssm_a_im[l], ssm_log_dt[l], ssm_b_re[l], ssm_b_im[l],
                             ssm_c_re[l], ssm_c_im[l], c // LANES)
        w = (ln_g[l], w_in[l].astype(BF16), ssm_w, ssm_d[l], w_glu[l].astype(BF16), b_glu[l],
             g_att[l], g_ssm[l], w_out[l].astype(BF16), final_g, l == depth - 1)
        xp, k1, v1, r1, i1 = _layer(xp, None, None, None, None, *w)
        xs, k2, v2, r2, i2 = _layer(xs, cache_k[l], cache_v[l], state_ssm_re[l], state_ssm_im[l], *w)
        for lst, val in zip((kp, vp, hrp, hip, ksl, vsl, hrs, his), (k1, v1, r1, i1, k2, v2, r2, i2)):
            lst.append(val)

    def heads(zs):
        z = jnp.stack(zs)
        return z.reshape(z.shape[:3] + (n_heads, HEAD_DIM))

    def states(zs):
        z = jnp.stack(zs)
        return z.reshape(z.shape[:2] + (n_groups, n_state))

    return (xp, xs, heads(kp), heads(vp), states(hrp), states(hip),
            heads(ksl), heads(vsl), states(hrs), states(his))
```

```python
import functools
import math

import jax
import jax.numpy as jnp
from jax import lax
from jax.experimental import pallas as pl
from jax.experimental.pallas import tpu as pltpu

F32 = jnp.float32
BF16 = jnp.bfloat16

EPS = 1e-6
HEAD_DIM = 64
SSM_GROUP = 16
LANES = 128
ATT_TK = 256
ATT_TQ = 256
ATT_GROUP = 2
DEAD_LOG2 = -160.0
MINUS_INF_LOG2 = -1e30
SSM_CHUNK = 64
ROW_TILE = 512
MIB = 1024 * 1024


def _params(sem, vmem_mib):
    return pltpu.CompilerParams(dimension_semantics=sem, vmem_limit_bytes=vmem_mib * MIB)


def _rms(x, g):
    ms = jnp.mean(x * x, axis=-1, keepdims=True)
    return x * lax.rsqrt(ms + EPS) * g


def _inproj_body(x_ref, g_ref, w_ref, *out_refs):
    hn = _rms(x_ref[...], g_ref[...]).astype(BF16)
    width = out_refs[0].shape[-1]
    for n, o_ref in enumerate(out_refs):
        o_ref[...] = jnp.dot(hn, w_ref[:, n * width:(n + 1) * width], preferred_element_type=F32)


def _inproj(x2d, g, w_bf, n_out):
    m, d = x2d.shape
    width = w_bf.shape[1] // n_out
    tm = min(ROW_TILE, m)
    return pl.pallas_call(
        _inproj_body,
        grid=(m // tm,),
        in_specs=[pl.BlockSpec((tm, d), lambda i: (i, 0)),
                  pl.BlockSpec((1, d), lambda i: (0, 0)),
                  pl.BlockSpec(w_bf.shape, lambda i: (0, 0))],
        out_specs=[pl.BlockSpec((tm, width), lambda i: (i, 0))] * n_out,
        out_shape=[jax.ShapeDtypeStruct((m, width), F32)] * n_out,
        compiler_params=_params(("parallel",), 48),
        name="inproj",
    )(x2d, g.reshape(1, d), w_bf)


def _attn_body(q_ref, k_ref, v_ref, o_ref, unfinished_ref, kb_ref, vb_ref, acc_ref, z_ref, ls_ref,
               after_ref, rs_ref, *, tq, tk, nq, q_off, group):
    kb_ref[...] = k_ref[0].astype(BF16)
    vb_ref[...] = v_ref[0].astype(BF16)
    rows = 2 * tq
    first_head = lax.broadcasted_iota(jnp.int32, (1, LANES), 1) < HEAD_DIM
    strict = (lax.broadcasted_iota(jnp.int32, (tk, tk), 0)
              > lax.broadcasted_iota(jnp.int32, (tk, tk), 1)).astype(BF16)
    qrow = lax.broadcasted_iota(jnp.int32, (rows, tk), 0) & (tq - 1)
    col = lax.broadcasted_iota(jnp.int32, (rows, tk), 1)
    scale = math.log2(math.e) / math.sqrt(HEAD_DIM)
    sign_bit = jnp.uint32(0x80000000)

    def key_rows(j):
        return pl.ds(pl.multiple_of(jnp.maximum(j, 0) * tk, tk), tk)

    def logits(qs, j):
        return lax.dot_general(qs, kb_ref[key_rows(j), :], (((1,), (1,)), ((), ())),
                               preferred_element_type=F32)

    def log_gates(z, mask):
        if mask is not None:
            z = jnp.where(mask, z, MINUS_INF_LOG2)
        neg_abs = lax.bitcast_convert_type(lax.bitcast_convert_type(z, jnp.uint32) | sign_bit, F32)
        ls = jnp.minimum(z, 0.0) - jnp.log2(1.0 + jnp.exp2(neg_abs))
        lr = ls - z
        return ls, lr

    def suffix(lr):
        return jnp.dot(lr.astype(BF16), strict, preferred_element_type=F32)

    def weighted(w, j):
        return jnp.dot(w.astype(BF16), vb_ref[key_rows(j), :], preferred_element_type=F32)

    def middle():
        ls, lr = log_gates(z_ref[...], None)
        ls_ref[...] = ls
        after_ref[...] = suffix(lr)
        rs_ref[...] = jnp.sum(lr, axis=1, keepdims=True)

    def finish(j, carry):
        pv = weighted(jnp.exp2(ls_ref[...] + after_ref[...] + carry), j)
        return pv, carry + rs_ref[...]

    unfinished_ref[...] = jnp.zeros(unfinished_ref.shape, F32)

    def mark_unfinished(flag):
        unfinished_ref[...] = jnp.maximum(unfinished_ref[...], jnp.where(flag, 1.0, 0.0))

    def q_group(g, _):
        tiles = []
        for n in range(group):
            i = g * group + n
            r0 = pl.multiple_of(i * tq, tq)
            q = q_ref[0, pl.ds(r0, tq), :] * scale
            qs = jnp.concatenate([jnp.where(first_head, q, 0.0), jnp.where(first_head, 0.0, q)],
                                 axis=0).astype(BF16)
            t_first = q_off + i * tq
            jmax = (t_first + tq - 1) // tk
            mask = (jmax * tk + col) < (t_first + qrow)
            tiles.append(dict(r0=r0, qs=qs, jmax=jmax, mask=mask,
                              z0=logits(qs, jmax), z1=logits(qs, jmax - 1)))
        for tl in tiles:
            tl["ls0"], lr0 = log_gates(tl.pop("z0"), tl["mask"])
            tl["after0"] = suffix(lr0)
            tl["rs0"] = jnp.sum(lr0, axis=1, keepdims=True)
            tl["ls1"], lr1 = log_gates(tl.pop("z1"), None)
            tl["after1"] = suffix(lr1)
            tl["rs1"] = jnp.sum(lr1, axis=1, keepdims=True)
        for n, tl in enumerate(tiles):
            jmax = tl["jmax"]
            w0 = jnp.exp2(tl["ls0"] + tl["after0"])
            no_tile1 = jnp.where(jmax >= 1, 0.0, MINUS_INF_LOG2)
            w1 = jnp.exp2(tl["ls1"] + tl["after1"] + (tl["rs0"] + no_tile1))
            acc_ref[n] = weighted(w0, jmax) + weighted(w1, jmax - 1)
            tl["carry"] = tl["rs0"] + jnp.where(jmax >= 1, tl["rs1"], 0.0)

        for n, tl in enumerate(tiles):
            qs, jmax, carry = tl["qs"], tl["jmax"], tl["carry"]
            alive = jnp.max(carry) > DEAD_LOG2
            mark_unfinished(jnp.logical_and(jmax < 2, alive))

            @pl.when(jnp.logical_and(jmax >= 2, alive))
            def _(n=n, qs=qs, jmax=jmax, carry=carry):
                z_ref[...] = logits(qs, jmax - 2)
                z_next = logits(qs, jmax - 3)
                middle()
                z_ref[...] = z_next

                def kstep(st):
                    t, carry, _ = st
                    z_next = logits(qs, jmax - t)
                    pv, carry = finish(jmax - t + 2, carry)
                    acc_ref[n] += pv
                    middle()
                    z_ref[...] = z_next
                    return t + 1, carry, jnp.max(carry) > DEAD_LOG2

                st = lax.while_loop(lambda st: jnp.logical_and(st[0] <= jmax + 2, st[2]), kstep,
                                    (jnp.int32(4), carry, jnp.bool_(True)))
                mark_unfinished(st[2])

            o_ref[0, pl.ds(tl["r0"], tq), :] = jnp.where(first_head, acc_ref[n, :tq, :],
                                                         acc_ref[n, tq:, :])
        return 0

    lax.fori_loop(0, nq // group, q_group, 0)


def _attention(q, k_all, v_all, q_off):
    b, t, c = q.shape
    s = k_all.shape[1]
    tq = min(ATT_TQ, t)
    assert t % tq == 0 and ATT_TK % tq == 0 and q_off % tq == 0 and s % ATT_TK == 0
    nq = t // tq
    group = ATT_GROUP if nq % ATT_GROUP == 0 else 1
    body = functools.partial(_attn_body, tq=tq, tk=ATT_TK, nq=nq, q_off=q_off, group=group)
    qspec = pl.BlockSpec((1, t, LANES), lambda bi, hp: (bi, 0, hp))
    kspec = pl.BlockSpec((1, s, LANES), lambda bi, hp: (bi, 0, hp))
    return pl.pallas_call(
        body,
        grid=(b, c // LANES),
        in_specs=[qspec, kspec, kspec],
        out_specs=[qspec, pl.BlockSpec((1, 1, 8, LANES), lambda bi, hp: (bi, hp, 0, 0))],
        out_shape=[jax.ShapeDtypeStruct((b, t, c), F32),
                   jax.ShapeDtypeStruct((b, c // LANES, 8, LANES), F32)],
        scratch_shapes=[pltpu.VMEM((s, LANES), BF16), pltpu.VMEM((s, LANES), BF16),
                        pltpu.VMEM((group, 2 * tq, LANES), F32),
                        pltpu.VMEM((2 * tq, ATT_TK), F32),
                        pltpu.VMEM((2 * tq, ATT_TK), F32),
                        pltpu.VMEM((2 * tq, ATT_TK), F32),
                        pltpu.VMEM((2 * tq, 1), F32)],
        compiler_params=_params(("parallel", "parallel"), 48),
        name="sb_attention",
    )(q, k_all, v_all)


def _ssm_body(u_ref, h0r_ref, h0i_ref, bd_ref, cm_ref, a_ref, d_ref, wg_ref, bg_ref,
              y_ref, hr_ref, hi_ref, h_sc, utb_sc, bu_sc, ytb_sc, *, chunk, batch, ntile):
    half = bd_ref.shape[2] // 2

    @pl.when(pl.program_id(0) == 0)
    def _():
        for j in range(ntile):
            h_sc[j, 0] = h0r_ref[:, j * half:(j + 1) * half]
            h_sc[j, 1] = h0i_ref[:, j * half:(j + 1) * half]

    for bi in range(batch):
        for j in range(ntile):
            utb_sc[j, pl.ds(bi, chunk, stride=batch), :] = u_ref[bi, :, j * LANES:(j + 1) * LANES]

    for j in range(ntile):
        bu_sc[...] = jnp.dot(utb_sc[j].astype(BF16), bd_ref[j], preferred_element_type=F32)
        ar = a_ref[j, 0:1, :]
        ai = a_ref[j, 1:2, :]

        def step(t, h):
            hr, hi = h
            r0 = pl.multiple_of(t * batch, batch)
            nr = ar * hr - ai * hi + bu_sc[pl.ds(r0, batch), :half]
            ni = ar * hi + ai * hr + bu_sc[pl.ds(r0, batch), half:]
            bu_sc[pl.ds(r0, batch), :half] = nr
            bu_sc[pl.ds(r0, batch), half:] = ni
            return nr, ni

        hr, hi = lax.fori_loop(0, chunk, step, (h_sc[j, 0], h_sc[j, 1]), unroll=True)
        h_sc[j, 0] = hr
        h_sc[j, 1] = hi
        hr_ref[:, j * half:(j + 1) * half] = hr
        hi_ref[:, j * half:(j + 1) * half] = hi
        ytb_sc[j] = jnp.dot(bu_sc[...].astype(BF16), cm_ref[j], preferred_element_type=F32)

    ch = jnp.concatenate([ytb_sc[j] for j in range(ntile)], axis=1)
    uu = jnp.concatenate([utb_sc[j] for j in range(ntile)], axis=1)
    y = jax.nn.gelu(ch + d_ref[...] * uu)
    gl = jnp.dot(y.astype(BF16), wg_ref[...], preferred_element_type=F32) + bg_ref[...]
    out = y * jax.nn.sigmoid(gl)
    for j in range(ntile):
        ytb_sc[j] = out[:, j * LANES:(j + 1) * LANES]
    for bi in range(batch):
        for j in range(ntile):
            y_ref[bi, :, j * LANES:(j + 1) * LANES] = ytb_sc[j, pl.ds(bi, chunk, stride=batch), :]


def _ssm(u, h0r, h0i, bd, cm, a, d, wg_bf, bg):
    b, t, c = u.shape
    ntile = c // LANES
    states = h0r.shape[1]
    chunk = min(SSM_CHUNK, t)
    body = functools.partial(_ssm_body, chunk=chunk, batch=b, ntile=ntile)
    const = lambda shape: pl.BlockSpec(shape, lambda i: (0,) * len(shape))
    return pl.pallas_call(
        body,
        grid=(t // chunk,),
        in_specs=[pl.BlockSpec((b, chunk, c), lambda i: (0, i, 0)),
                  const((b, states)), const((b, states)),
                  const(bd.shape), const(cm.shape), const(a.shape),
                  const((1, c)), const(wg_bf.shape), const((1, c))],
        out_specs=[pl.BlockSpec((b, chunk, c), lambda i: (0, i, 0)),
                   const((b, states)), const((b, states))],
        out_shape=[jax.ShapeDtypeStruct((b, t, c), F32),
                   jax.ShapeDtypeStruct((b, states), F32),
                   jax.ShapeDtypeStruct((b, states), F32)],
        scratch_shapes=[pltpu.VMEM((ntile, 2, b, states // ntile), F32),
                        pltpu.VMEM((ntile, chunk * b, LANES), F32),
                        pltpu.VMEM((chunk * b, 2 * states // ntile), F32),
                        pltpu.VMEM((ntile, chunk * b, LANES), F32)],
        compiler_params=_params(("arbitrary",), 48),
        name="s5_branch",
    )(u, h0r, h0i, bd, cm, a, d.reshape(1, c), wg_bf, bg.reshape(1, c))


def _ssm_weights(a_re, a_im, log_dt, b_re, b_im, c_re, c_im, ntile):
    g, p = a_re.shape
    gt = g // ntile
    dt = jnp.exp(log_dt)[:, None]
    mag = jnp.exp(a_re * dt)
    ang = a_im * dt
    ab_re = mag * jnp.cos(ang)
    ab_im = mag * jnp.sin(ang)
    den = a_re * a_re + a_im * a_im
    f_re = ((ab_re - 1.0) * a_re + ab_im * a_im) / den
    f_im = (ab_im * a_re - (ab_re - 1.0) * a_im) / den
    bb_re = f_re[..., None] * b_re - f_im[..., None] * b_im
    bb_im = f_re[..., None] * b_im + f_im[..., None] * b_re
    eye = jnp.eye(gt, dtype=F32)

    def expand(bb):
        bb = bb.reshape(ntile, gt, p, SSM_GROUP)
        return jnp.einsum("jgpc,gh->jgchp", bb, eye).reshape(ntile, gt * SSM_GROUP, gt * p)

    def contract(cc):
        cc = cc.reshape(ntile, gt, SSM_GROUP, p)
        return jnp.einsum("jgcp,gh->jhpgc", cc, eye).reshape(ntile, gt * p, gt * SSM_GROUP)

    bd = jnp.concatenate([expand(bb_re), expand(bb_im)], axis=2).astype(BF16)
    cm = jnp.concatenate([contract(c_re), -contract(c_im)], axis=1).astype(BF16)
    a = jnp.stack([ab_re.reshape(ntile, gt * p), ab_im.reshape(ntile, gt * p)], axis=1)
    return bd, cm, a


def _outproj_body(att_ref, ga_ref, sy_ref, gs_ref, x_ref, gatt_ref, gssm_ref, w_ref, gout_ref,
                  o_ref, *, norm_out):
    ga = ga_ref[...]
    gs = gs_ref[...]
    a = (_rms(att_ref[...], gatt_ref[...]) * (ga * jax.nn.sigmoid(ga))).astype(BF16)
    s = (_rms(sy_ref[...], gssm_ref[...]) * (gs * jax.nn.sigmoid(gs))).astype(BF16)
    d_att = a.shape[1]
    out = (jnp.dot(a, w_ref[:d_att, :], preferred_element_type=F32)
           + jnp.dot(s, w_ref[d_att:, :], preferred_element_type=F32))
    x_new = x_ref[...] + out
    o_ref[...] = _rms(x_new, gout_ref[...]) if norm_out else x_new


def _outproj(att, ga, sy, gs, x2d, g_att, g_ssm, w_bf, g_out, norm_out):
    m, d = x2d.shape
    c = att.shape[1]
    tm = min(ROW_TILE, m)
    half = pl.BlockSpec((tm, c), lambda i: (i, 0))
    return pl.pallas_call(
        functools.partial(_outproj_body, norm_out=norm_out),
        grid=(m // tm,),
        in_specs=[half, half, half, half,
                  pl.BlockSpec((tm, d), lambda i: (i, 0)),
                  pl.BlockSpec((1, c), lambda i: (0, 0)),
                  pl.BlockSpec((1, c), lambda i: (0, 0)),
                  pl.BlockSpec(w_bf.shape, lambda i: (0, 0)),
                  pl.BlockSpec((1, d), lambda i: (0, 0))],
        out_specs=pl.BlockSpec((tm, d), lambda i: (i, 0)),
        out_shape=jax.ShapeDtypeStruct((m, d), F32),
        compiler_params=_params(("parallel",), 48),
        name="outproj",
    )(att, ga, sy, gs, x2d, g_att.reshape(1, c), g_ssm.reshape(1, c), w_bf, g_out.reshape(1, d))


def _cache_rows_body(x_ref, o_ref):
    for h in range(x_ref.shape[3]):
        o_ref[0, :, h * HEAD_DIM:(h + 1) * HEAD_DIM] = x_ref[0, 0, :, h, :]


def _cache_rows(cache, layer):
    _, b, p, h, dh = cache.shape
    rows = min(ATT_TK, p)
    assert p % rows == 0 and dh == HEAD_DIM
    return pl.pallas_call(
        _cache_rows_body,
        grid=(b, p // rows),
        in_specs=[pl.BlockSpec((1, 1, rows, h, dh), lambda bi, j: (layer, bi, j, 0, 0))],
        out_specs=pl.BlockSpec((1, rows, h * dh), lambda bi, j: (bi, j, 0)),
        out_shape=jax.ShapeDtypeStruct((b, p, h * dh), F32),
        compiler_params=_params(("parallel", "parallel"), 32),
        name="cache_rows",
    )(cache)


def _sample_attention(q3, k3, v3, cache_k, cache_v, layer):
    b, t, c = q3.shape
    p = cache_k.shape[2]

    def with_past(k_rows, v_rows):
        n = k_rows.shape[1]
        zeros = jnp.zeros((b, (-(n + t)) % ATT_TK, c), F32)
        return _attention(q3, jnp.concatenate([k_rows, k3, zeros], axis=1),
                          jnp.concatenate([v_rows, v3, zeros], axis=1), n)

    def whole_past():
        return with_past(_cache_rows(cache_k, layer), _cache_rows(cache_v, layer))[0]

    if p <= ATT_TK:
        return whole_past()
    att, unfinished = with_past(cache_k[layer, :, p - ATT_TK:].reshape(b, ATT_TK, c),
                                cache_v[layer, :, p - ATT_TK:].reshape(b, ATT_TK, c))
    return lax.cond(jnp.max(unfinished) > 0.0, whole_past, lambda: att)


def _layer(x, past, h0_re, h0_im, ln_g, w_in_bf, ssm_w, d_skip, w_glu_bf, b_glu,
           g_att, g_ssm, w_out_bf, g_out, norm_out):
    b, t, d = x.shape
    x2d = x.reshape(b * t, d)
    q, k, v, ga, u, gs = _inproj(x2d, ln_g, w_in_bf, 6)
    c = q.shape[1]
    q3, k3, v3 = (z.reshape(b, t, c) for z in (q, k, v))
    if past is None:
        att, _ = _attention(q3, k3, v3, 0)
    else:
        att = _sample_attention(q3, k3, v3, *past)
    bd, cm, a = ssm_w
    states = a.shape[0] * a.shape[2]
    if h0_re is None:
        h0_re = jnp.zeros((b, states), F32)
        h0_im = jnp.zeros((b, states), F32)
    sy, h_re, h_im = _ssm(u.reshape(b, t, c), h0_re.reshape(b, states), h0_im.reshape(b, states),
                          bd, cm, a, d_skip, w_glu_bf, b_glu)
    x_new = _outproj(att.reshape(b * t, c), ga, sy.reshape(b * t, c), gs, x2d, g_att, g_ssm, w_out_bf,
                     g_out, norm_out)
    return x_new.reshape(b, t, d), k3, v3, h_re, h_im


def kernel(x_prompt, x_sample, cache_k, cache_v, state_ssm_re, state_ssm_im, ln_g, w_in, ssm_a_re, ssm_a_im, ssm_log_dt, ssm_b_re, ssm_b_im, ssm_c_re, ssm_c_im, ssm_d, w_glu, b_glu, g_att, g_ssm, w_out, final_g):
    depth = w_in.shape[0]
    n_groups, n_state = ssm_a_re.shape[1:]
    c = ssm_d.shape[1]
    n_heads = c // HEAD_DIM
    xp, xs = x_prompt, x_sample
    kp, vp, hrp, hip, ksl, vsl, hrs, his = ([] for _ in range(8))
    for l in range(depth):
        ssm_w = _ssm_weights(ssm_a_re[l], ssm_a_im[l], ssm_log_dt[l], ssm_b_re[l], ssm_b_im[l],
                             ssm_c_re[l], ssm_c_im[l], c // LANES)
        w = (ln_g[l], w_in[l].astype(BF16), ssm_w, ssm_d[l], w_glu[l].astype(BF16), b_glu[l],
             g_att[l], g_ssm[l], w_out[l].astype(BF16), final_g, l == depth - 1)
        xp, k1, v1, r1, i1 = _layer(xp, None, None, None, *w)
        xs, k2, v2, r2, i2 = _layer(xs, (cache_k, cache_v, l), state_ssm_re[l], state_ssm_im[l], *w)
        for lst, val in zip((kp, vp, hrp, hip, ksl, vsl, hrs, his), (k1, v1, r1, i1, k2, v2, r2, i2)):
            lst.append(val)

    def heads(zs):
        z = jnp.stack(zs)
        return z.reshape(z.shape[:3] + (n_heads, HEAD_DIM))

    def states(zs):
        z = jnp.stack(zs)
        return z.reshape(z.shape[:2] + (n_groups, n_state))

    return (xp, xs, heads(kp), heads(vp), states(hrp), states(hip),
            heads(ksl), heads(vsl), states(hrs), states(his))
```

```python
import functools
import math

import jax
import jax.numpy as jnp
from jax import lax
from jax.experimental import pallas as pl
from jax.experimental.pallas import tpu as pltpu

F32 = jnp.float32
BF16 = jnp.bfloat16

EPS = 1e-6
HEAD_DIM = 64
SSM_GROUP = 16
LANES = 128
ATT_TK = 256
ATT_TQ = 256
ATT_GROUP = 2
DEAD_LOG2 = -160.0
MINUS_INF_LOG2 = -1e30
SSM_CHUNK = 64
ROW_TILE = 512
MIB = 1024 * 1024


def _params(sem, vmem_mib):
    return pltpu.CompilerParams(dimension_semantics=sem, vmem_limit_bytes=vmem_mib * MIB)


def _rms(x, g):
    ms = jnp.mean(x * x, axis=-1, keepdims=True)
    return x * lax.rsqrt(ms + EPS) * g


def _inproj_body(x_ref, g_ref, w_ref, *refs, n_out, heads_t):
    hn = _rms(x_ref[...], g_ref[...]).astype(BF16)
    out_refs = refs[2:2 + n_out] if heads_t else refs
    width = out_refs[0].shape[-1]
    for n, o_ref in enumerate(out_refs):
        y = jnp.dot(hn, w_ref[:, n * width:(n + 1) * width], preferred_element_type=F32)
        o_ref[...] = y
        if heads_t and n in (1, 2):
            t_ref = refs[2 + n_out + n - 1]
            t_ref[0, 0] = y.T.reshape(t_ref.shape[2:])


def _inproj(x2d, g, w_bf, n_out, stacks=None):
    m, d = x2d.shape
    width = w_bf.shape[1] // n_out
    tm = min(ROW_TILE, m)
    in_specs = [pl.BlockSpec((tm, d), lambda i: (i, 0)),
                pl.BlockSpec((1, d), lambda i: (0, 0)),
                pl.BlockSpec(w_bf.shape, lambda i: (0, 0))]
    out_specs = [pl.BlockSpec((tm, width), lambda i: (i, 0))] * n_out
    out_shape = [jax.ShapeDtypeStruct((m, width), F32)] * n_out
    args = [x2d, g.reshape(1, d), w_bf]
    aliases = {}
    if stacks is not None:
        kt, vt, layer, seq = stacks
        tiles = seq // tm
        assert seq % tm == 0 and kt.shape[2] * kt.shape[3] == width
        stack_spec = pl.BlockSpec((1, 1) + kt.shape[2:4] + (tm,),
                                  lambda i: (layer, i // tiles, 0, 0, i % tiles))
        in_specs += [pl.BlockSpec(memory_space=pl.ANY)] * 2
        args += [kt, vt]
        out_specs += [stack_spec, stack_spec]
        out_shape += [jax.ShapeDtypeStruct(kt.shape, F32)] * 2
        aliases = {3: n_out, 4: n_out + 1}
    return pl.pallas_call(
        functools.partial(_inproj_body, n_out=n_out, heads_t=stacks is not None),
        grid=(m // tm,),
        in_specs=in_specs,
        out_specs=out_specs,
        out_shape=out_shape,
        input_output_aliases=aliases,
        compiler_params=_params(("parallel",), 56),
        name="inproj",
    )(*args)


def _attn_body(q_ref, k_ref, v_ref, o_ref, unfinished_ref, kb_ref, vb_ref, acc_ref, z_ref, ls_ref,
               after_ref, rs_ref, *, tq, tk, nq, q_off, group):
    kb_ref[...] = k_ref[0].astype(BF16)
    vb_ref[...] = v_ref[0].astype(BF16)
    rows = 2 * tq
    first_head = lax.broadcasted_iota(jnp.int32, (1, LANES), 1) < HEAD_DIM
    strict = (lax.broadcasted_iota(jnp.int32, (tk, tk), 0)
              > lax.broadcasted_iota(jnp.int32, (tk, tk), 1)).astype(BF16)
    qrow = lax.broadcasted_iota(jnp.int32, (rows, tk), 0) & (tq - 1)
    col = lax.broadcasted_iota(jnp.int32, (rows, tk), 1)
    scale = math.log2(math.e) / math.sqrt(HEAD_DIM)
    sign_bit = jnp.uint32(0x80000000)

    def key_rows(j):
        return pl.ds(pl.multiple_of(jnp.maximum(j, 0) * tk, tk), tk)

    def logits(qs, j):
        return lax.dot_general(qs, kb_ref[key_rows(j), :], (((1,), (1,)), ((), ())),
                               preferred_element_type=F32)

    def log_gates(z, mask):
        if mask is not None:
            z = jnp.where(mask, z, MINUS_INF_LOG2)
        neg_abs = lax.bitcast_convert_type(lax.bitcast_convert_type(z, jnp.uint32) | sign_bit, F32)
        ls = jnp.minimum(z, 0.0) - jnp.log2(1.0 + jnp.exp2(neg_abs))
        lr = ls - z
        return ls, lr

    def suffix(lr):
        return jnp.dot(lr.astype(BF16), strict, preferred_element_type=F32)

    def weighted(w, j):
        return jnp.dot(w.astype(BF16), vb_ref[key_rows(j), :], preferred_element_type=F32)

    def middle():
        ls, lr = log_gates(z_ref[...], None)
        ls_ref[...] = ls
        after_ref[...] = suffix(lr)
        rs_ref[...] = jnp.sum(lr, axis=1, keepdims=True)

    def finish(j, carry):
        pv = weighted(jnp.exp2(ls_ref[...] + after_ref[...] + carry), j)
        return pv, carry + rs_ref[...]

    unfinished_ref[...] = jnp.zeros(unfinished_ref.shape, F32)

    def mark_unfinished(flag):
        unfinished_ref[...] = jnp.maximum(unfinished_ref[...], jnp.where(flag, 1.0, 0.0))

    def q_group(g, _):
        tiles = []
        for n in range(group):
            i = g * group + n
            r0 = pl.multiple_of(i * tq, tq)
            q = q_ref[0, pl.ds(r0, tq), :] * scale
            qs = jnp.concatenate([jnp.where(first_head, q, 0.0), jnp.where(first_head, 0.0, q)],
                                 axis=0).astype(BF16)
            t_first = q_off + i * tq
            jmax = (t_first + tq - 1) // tk
            mask = (jmax * tk + col) < (t_first + qrow)
            tiles.append(dict(r0=r0, qs=qs, jmax=jmax, mask=mask,
                              z0=logits(qs, jmax), z1=logits(qs, jmax - 1)))
        for tl in tiles:
            tl["ls0"], lr0 = log_gates(tl.pop("z0"), tl["mask"])
            tl["after0"] = suffix(lr0)
            tl["rs0"] = jnp.sum(lr0, axis=1, keepdims=True)
            tl["ls1"], lr1 = log_gates(tl.pop("z1"), None)
            tl["after1"] = suffix(lr1)
            tl["rs1"] = jnp.sum(lr1, axis=1, keepdims=True)
        for n, tl in enumerate(tiles):
            jmax = tl["jmax"]
            w0 = jnp.exp2(tl["ls0"] + tl["after0"])
            no_tile1 = jnp.where(jmax >= 1, 0.0, MINUS_INF_LOG2)
            w1 = jnp.exp2(tl["ls1"] + tl["after1"] + (tl["rs0"] + no_tile1))
            acc_ref[n] = weighted(w0, jmax) + weighted(w1, jmax - 1)
            tl["carry"] = tl["rs0"] + jnp.where(jmax >= 1, tl["rs1"], 0.0)

        for n, tl in enumerate(tiles):
            qs, jmax, carry = tl["qs"], tl["jmax"], tl["carry"]
            alive = jnp.max(carry) > DEAD_LOG2
            mark_unfinished(jnp.logical_and(jmax < 2, alive))

            @pl.when(jnp.logical_and(jmax >= 2, alive))
            def _(n=n, qs=qs, jmax=jmax, carry=carry):
                z_ref[...] = logits(qs, jmax - 2)
                z_next = logits(qs, jmax - 3)
                middle()
                z_ref[...] = z_next

                def kstep(st):
                    t, carry, _ = st
                    z_next = logits(qs, jmax - t)
                    pv, carry = finish(jmax - t + 2, carry)
                    acc_ref[n] += pv
                    middle()
                    z_ref[...] = z_next
                    return t + 1, carry, jnp.max(carry) > DEAD_LOG2

                st = lax.while_loop(lambda st: jnp.logical_and(st[0] <= jmax + 2, st[2]), kstep,
                                    (jnp.int32(4), carry, jnp.bool_(True)))
                mark_unfinished(st[2])

            o_ref[0, pl.ds(tl["r0"], tq), :] = jnp.where(first_head, acc_ref[n, :tq, :],
                                                         acc_ref[n, tq:, :])
        return 0

    lax.fori_loop(0, nq // group, q_group, 0)


def _attention(q, k_all, v_all, q_off):
    b, t, c = q.shape
    s = k_all.shape[1]
    tq = min(ATT_TQ, t)
    assert t % tq == 0 and ATT_TK % tq == 0 and q_off % tq == 0 and s % ATT_TK == 0
    nq = t // tq
    group = ATT_GROUP if nq % ATT_GROUP == 0 else 1
    body = functools.partial(_attn_body, tq=tq, tk=ATT_TK, nq=nq, q_off=q_off, group=group)
    qspec = pl.BlockSpec((1, t, LANES), lambda bi, hp: (bi, 0, hp))
    kspec = pl.BlockSpec((1, s, LANES), lambda bi, hp: (bi, 0, hp))
    return pl.pallas_call(
        body,
        grid=(b, c // LANES),
        in_specs=[qspec, kspec, kspec],
        out_specs=[qspec, pl.BlockSpec((1, 1, 8, LANES), lambda bi, hp: (bi, hp, 0, 0))],
        out_shape=[jax.ShapeDtypeStruct((b, t, c), F32),
                   jax.ShapeDtypeStruct((b, c // LANES, 8, LANES), F32)],
        scratch_shapes=[pltpu.VMEM((s, LANES), BF16), pltpu.VMEM((s, LANES), BF16),
                        pltpu.VMEM((group, 2 * tq, LANES), F32),
                        pltpu.VMEM((2 * tq, ATT_TK), F32),
                        pltpu.VMEM((2 * tq, ATT_TK), F32),
                        pltpu.VMEM((2 * tq, ATT_TK), F32),
                        pltpu.VMEM((2 * tq, 1), F32)],
        compiler_params=_params(("parallel", "parallel"), 48),
        name="sb_attention",
    )(q, k_all, v_all)


def _ssm_body(u_ref, h0r_ref, h0i_ref, bd_ref, cm_ref, a_ref, d_ref, wg_ref, bg_ref,
              y_ref, hr_ref, hi_ref, h_sc, utb_sc, bu_sc, ytb_sc, *, chunk, batch, ntile):
    half = bd_ref.shape[2] // 2

    @pl.when(pl.program_id(0) == 0)
    def _():
        for j in range(ntile):
            h_sc[j, 0] = h0r_ref[:, j * half:(j + 1) * half]
            h_sc[j, 1] = h0i_ref[:, j * half:(j + 1) * half]

    for bi in range(batch):
        for j in range(ntile):
            utb_sc[j, pl.ds(bi, chunk, stride=batch), :] = u_ref[bi, :, j * LANES:(j + 1) * LANES]

    for j in range(ntile):
        bu_sc[...] = jnp.dot(utb_sc[j].astype(BF16), bd_ref[j], preferred_element_type=F32)
        ar = a_ref[j, 0:1, :]
        ai = a_ref[j, 1:2, :]

        def step(t, h):
            hr, hi = h
            r0 = pl.multiple_of(t * batch, batch)
            nr = ar * hr - ai * hi + bu_sc[pl.ds(r0, batch), :half]
            ni = ar * hi + ai * hr + bu_sc[pl.ds(r0, batch), half:]
            bu_sc[pl.ds(r0, batch), :half] = nr
            bu_sc[pl.ds(r0, batch), half:] = ni
            return nr, ni

        hr, hi = lax.fori_loop(0, chunk, step, (h_sc[j, 0], h_sc[j, 1]), unroll=True)
        h_sc[j, 0] = hr
        h_sc[j, 1] = hi
        hr_ref[:, j * half:(j + 1) * half] = hr
        hi_ref[:, j * half:(j + 1) * half] = hi
        ytb_sc[j] = jnp.dot(bu_sc[...].astype(BF16), cm_ref[j], preferred_element_type=F32)

    ch = jnp.concatenate([ytb_sc[j] for j in range(ntile)], axis=1)
    uu = jnp.concatenate([utb_sc[j] for j in range(ntile)], axis=1)
    y = jax.nn.gelu(ch + d_ref[...] * uu)
    gl = jnp.dot(y.astype(BF16), wg_ref[...], preferred_element_type=F32) + bg_ref[...]
    out = y * jax.nn.sigmoid(gl)
    for j in range(ntile):
        ytb_sc[j] = out[:, j * LANES:(j + 1) * LANES]
    for bi in range(batch):
        for j in range(ntile):
            y_ref[bi, :, j * LANES:(j + 1) * LANES] = ytb_sc[j, pl.ds(bi, chunk, stride=batch), :]


def _ssm(u, h0r, h0i, bd, cm, a, d, wg_bf, bg):
    b, t, c = u.shape
    ntile = c // LANES
    states = h0r.shape[1]
    chunk = min(SSM_CHUNK, t)
    body = functools.partial(_ssm_body, chunk=chunk, batch=b, ntile=ntile)
    const = lambda shape: pl.BlockSpec(shape, lambda i: (0,) * len(shape))
    return pl.pallas_call(
        body,
        grid=(t // chunk,),
        in_specs=[pl.BlockSpec((b, chunk, c), lambda i: (0, i, 0)),
                  const((b, states)), const((b, states)),
                  const(bd.shape), const(cm.shape), const(a.shape),
                  const((1, c)), const(wg_bf.shape), const((1, c))],
        out_specs=[pl.BlockSpec((b, chunk, c), lambda i: (0, i, 0)),
                   const((b, states)), const((b, states))],
        out_shape=[jax.ShapeDtypeStruct((b, t, c), F32),
                   jax.ShapeDtypeStruct((b, states), F32),
                   jax.ShapeDtypeStruct((b, states), F32)],
        scratch_shapes=[pltpu.VMEM((ntile, 2, b, states // ntile), F32),
                        pltpu.VMEM((ntile, chunk * b, LANES), F32),
                        pltpu.VMEM((chunk * b, 2 * states // ntile), F32),
                        pltpu.VMEM((ntile, chunk * b, LANES), F32)],
        compiler_params=_params(("arbitrary",), 48),
        name="s5_branch",
    )(u, h0r, h0i, bd, cm, a, d.reshape(1, c), wg_bf, bg.reshape(1, c))


def _ssm_weights(a_re, a_im, log_dt, b_re, b_im, c_re, c_im, ntile):
    g, p = a_re.shape
    gt = g // ntile
    dt = jnp.exp(log_dt)[:, None]
    mag = jnp.exp(a_re * dt)
    ang = a_im * dt
    ab_re = mag * jnp.cos(ang)
    ab_im = mag * jnp.sin(ang)
    den = a_re * a_re + a_im * a_im
    f_re = ((ab_re - 1.0) * a_re + ab_im * a_im) / den
    f_im = (ab_im * a_re - (ab_re - 1.0) * a_im) / den
    bb_re = f_re[..., None] * b_re - f_im[..., None] * b_im
    bb_im = f_re[..., None] * b_im + f_im[..., None] * b_re
    eye = jnp.eye(gt, dtype=F32)

    def expand(bb):
        bb = bb.reshape(ntile, gt, p, SSM_GROUP)
        return jnp.einsum("jgpc,gh->jgchp", bb, eye).reshape(ntile, gt * SSM_GROUP, gt * p)

    def contract(cc):
        cc = cc.reshape(ntile, gt, SSM_GROUP, p)
        return jnp.einsum("jgcp,gh->jhpgc", cc, eye).reshape(ntile, gt * p, gt * SSM_GROUP)

    bd = jnp.concatenate([expand(bb_re), expand(bb_im)], axis=2).astype(BF16)
    cm = jnp.concatenate([contract(c_re), -contract(c_im)], axis=1).astype(BF16)
    a = jnp.stack([ab_re.reshape(ntile, gt * p), ab_im.reshape(ntile, gt * p)], axis=1)
    return bd, cm, a


def _outproj_body(att_ref, ga_ref, sy_ref, gs_ref, x_ref, gatt_ref, gssm_ref, w_ref, gout_ref,
                  o_ref, *, norm_out):
    ga = ga_ref[...]
    gs = gs_ref[...]
    a = (_rms(att_ref[...], gatt_ref[...]) * (ga * jax.nn.sigmoid(ga))).astype(BF16)
    s = (_rms(sy_ref[...], gssm_ref[...]) * (gs * jax.nn.sigmoid(gs))).astype(BF16)
    d_att = a.shape[1]
    out = (jnp.dot(a, w_ref[:d_att, :], preferred_element_type=F32)
           + jnp.dot(s, w_ref[d_att:, :], preferred_element_type=F32))
    x_new = x_ref[...] + out
    o_ref[...] = _rms(x_new, gout_ref[...]) if norm_out else x_new


def _outproj(att, ga, sy, gs, x2d, g_att, g_ssm, w_bf, g_out, norm_out):
    m, d = x2d.shape
    c = att.shape[1]
    tm = min(ROW_TILE, m)
    half = pl.BlockSpec((tm, c), lambda i: (i, 0))
    return pl.pallas_call(
        functools.partial(_outproj_body, norm_out=norm_out),
        grid=(m // tm,),
        in_specs=[half, half, half, half,
                  pl.BlockSpec((tm, d), lambda i: (i, 0)),
                  pl.BlockSpec((1, c), lambda i: (0, 0)),
                  pl.BlockSpec((1, c), lambda i: (0, 0)),
                  pl.BlockSpec(w_bf.shape, lambda i: (0, 0)),
                  pl.BlockSpec((1, d), lambda i: (0, 0))],
        out_specs=pl.BlockSpec((tm, d), lambda i: (i, 0)),
        out_shape=jax.ShapeDtypeStruct((m, d), F32),
        compiler_params=_params(("parallel",), 48),
        name="outproj",
    )(att, ga, sy, gs, x2d, g_att.reshape(1, c), g_ssm.reshape(1, c), w_bf, g_out.reshape(1, d))


def _cache_rows_body(x_ref, o_ref):
    for h in range(x_ref.shape[3]):
        o_ref[0, :, h * HEAD_DIM:(h + 1) * HEAD_DIM] = x_ref[0, 0, :, h, :]


def _cache_rows(cache, layer):
    _, b, p, h, dh = cache.shape
    rows = min(ATT_TK, p)
    assert p % rows == 0 and dh == HEAD_DIM
    return pl.pallas_call(
        _cache_rows_body,
        grid=(b, p // rows),
        in_specs=[pl.BlockSpec((1, 1, rows, h, dh), lambda bi, j: (layer, bi, j, 0, 0))],
        out_specs=pl.BlockSpec((1, rows, h * dh), lambda bi, j: (bi, j, 0)),
        out_shape=jax.ShapeDtypeStruct((b, p, h * dh), F32),
        compiler_params=_params(("parallel", "parallel"), 32),
        name="cache_rows",
    )(cache)


def _sample_attention(q3, k3, v3, cache_k, cache_v, layer):
    b, t, c = q3.shape
    p = cache_k.shape[2]

    def with_past(k_rows, v_rows):
        n = k_rows.shape[1]
        zeros = jnp.zeros((b, (-(n + t)) % ATT_TK, c), F32)
        return _attention(q3, jnp.concatenate([k_rows, k3, zeros], axis=1),
                          jnp.concatenate([v_rows, v3, zeros], axis=1), n)

    def whole_past():
        return with_past(_cache_rows(cache_k, layer), _cache_rows(cache_v, layer))[0]

    if p <= ATT_TK:
        return whole_past()
    att, unfinished = with_past(cache_k[layer, :, p - ATT_TK:].reshape(b, ATT_TK, c),
                                cache_v[layer, :, p - ATT_TK:].reshape(b, ATT_TK, c))
    return lax.cond(jnp.max(unfinished) > 0.0, whole_past, lambda: att)


def _layer(x, past, stacks, h0_re, h0_im, ln_g, w_in_bf, ssm_w, d_skip, w_glu_bf, b_glu,
           g_att, g_ssm, w_out_bf, g_out, norm_out):
    b, t, d = x.shape
    x2d = x.reshape(b * t, d)
    if stacks is None:
        q, k, v, ga, u, gs = _inproj(x2d, ln_g, w_in_bf, 6)
    else:
        q, k, v, ga, u, gs, kt, vt = _inproj(x2d, ln_g, w_in_bf, 6, stacks + (t,))
    c = q.shape[1]
    q3, k3, v3 = (z.reshape(b, t, c) for z in (q, k, v))
    if past is None:
        att, _ = _attention(q3, k3, v3, 0)
    else:
        att = _sample_attention(q3, k3, v3, *past)
    bd, cm, a = ssm_w
    states = a.shape[0] * a.shape[2]
    if h0_re is None:
        h0_re = jnp.zeros((b, states), F32)
        h0_im = jnp.zeros((b, states), F32)
    sy, h_re, h_im = _ssm(u.reshape(b, t, c), h0_re.reshape(b, states), h0_im.reshape(b, states),
                          bd, cm, a, d_skip, w_glu_bf, b_glu)
    x_new = _outproj(att.reshape(b * t, c), ga, sy.reshape(b * t, c), gs, x2d, g_att, g_ssm, w_out_bf,
                     g_out, norm_out)
    return x_new.reshape(b, t, d), ((k3, v3) if stacks is None else (kt, vt)), h_re, h_im


def kernel(x_prompt, x_sample, cache_k, cache_v, state_ssm_re, state_ssm_im, ln_g, w_in, ssm_a_re, ssm_a_im, ssm_log_dt, ssm_b_re, ssm_b_im, ssm_c_re, ssm_c_im, ssm_d, w_glu, b_glu, g_att, g_ssm, w_out, final_g):
    depth = w_in.shape[0]
    n_groups, n_state = ssm_a_re.shape[1:]
    c = ssm_d.shape[1]
    n_heads = c // HEAD_DIM
    xp, xs = x_prompt, x_sample
    bp, tp, _ = xp.shape
    kt = jnp.zeros((depth, bp, n_heads, HEAD_DIM, tp), F32)
    vt = jnp.zeros((depth, bp, n_heads, HEAD_DIM, tp), F32)
    hrp, hip, ksl, vsl, hrs, his = ([] for _ in range(6))
    for l in range(depth):
        ssm_w = _ssm_weights(ssm_a_re[l], ssm_a_im[l], ssm_log_dt[l], ssm_b_re[l], ssm_b_im[l],
                             ssm_c_re[l], ssm_c_im[l], c // LANES)
        w = (ln_g[l], w_in[l].astype(BF16), ssm_w, ssm_d[l], w_glu[l].astype(BF16), b_glu[l],
             g_att[l], g_ssm[l], w_out[l].astype(BF16), final_g, l == depth - 1)
        xp, (kt, vt), r1, i1 = _layer(xp, None, (kt, vt, l), None, None, *w)
        xs, (k2, v2), r2, i2 = _layer(xs, (cache_k, cache_v, l), None, state_ssm_re[l], state_ssm_im[l], *w)
        for lst, val in zip((hrp, hip, ksl, vsl, hrs, his), (r1, i1, k2, v2, r2, i2)):
            lst.append(val)

    def time_major(zt):
        return zt.transpose(0, 1, 4, 2, 3)

    def heads(zs):
        z = jnp.stack(zs)
        return z.reshape(z.shape[:3] + (n_heads, HEAD_DIM))

    def states(zs):
        z = jnp.stack(zs)
        return z.reshape(z.shape[:2] + (n_groups, n_state))

    return (xp, xs, time_major(kt), time_major(vt), states(hrp), states(hip),
            heads(ksl), heads(vsl), states(hrs), states(his))
```

```python
import functools
import math

import jax
import jax.numpy as jnp
from jax import lax
from jax.experimental import pallas as pl
from jax.experimental.pallas import tpu as pltpu

F32 = jnp.float32
BF16 = jnp.bfloat16

EPS = 1e-6
HEAD_DIM = 64
SSM_GROUP = 16
LANES = 128
ATT_TK = 256
ATT_TQ = 256
ATT_GROUP = 2
DEAD_LOG2 = -160.0
ATT_Q_SCALE = math.log2(math.e) / math.sqrt(HEAD_DIM)
MINUS_INF_LOG2 = -1e30
SSM_CHUNK = 64
ROW_TILE = 512
MIB = 1024 * 1024


def _params(sem, vmem_mib):
    return pltpu.CompilerParams(dimension_semantics=sem, vmem_limit_bytes=vmem_mib * MIB)


def _rms(x, g):
    ms = jnp.mean(x * x, axis=-1, keepdims=True)
    return x * lax.rsqrt(ms + EPS) * g


def _inproj_body(x_ref, g_ref, w_ref, *refs, n_out, heads_t, scale0):
    hn = _rms(x_ref[...], g_ref[...]).astype(BF16)
    out_refs = refs[2:2 + n_out] if heads_t else refs
    width = out_refs[0].shape[-1]
    for n, o_ref in enumerate(out_refs):
        y = jnp.dot(hn, w_ref[:, n * width:(n + 1) * width], preferred_element_type=F32)
        o_ref[...] = (y * scale0 if n == 0 else y).astype(o_ref.dtype)
        if heads_t and n in (1, 2):
            t_ref = refs[2 + n_out + n - 1]
            t_ref[0, 0] = y.T.reshape(t_ref.shape[2:])


def _inproj(x2d, g, w_bf, out_dtypes, scale0, stacks=None):
    m, d = x2d.shape
    n_out = len(out_dtypes)
    width = w_bf.shape[1] // n_out
    tm = min(ROW_TILE, m)
    in_specs = [pl.BlockSpec((tm, d), lambda i: (i, 0)),
                pl.BlockSpec((1, d), lambda i: (0, 0)),
                pl.BlockSpec(w_bf.shape, lambda i: (0, 0))]
    out_specs = [pl.BlockSpec((tm, width), lambda i: (i, 0))] * n_out
    out_shape = [jax.ShapeDtypeStruct((m, width), dt) for dt in out_dtypes]
    args = [x2d, g.reshape(1, d), w_bf]
    aliases = {}
    if stacks is not None:
        kt, vt, layer, seq = stacks
        tiles = seq // tm
        assert seq % tm == 0 and kt.shape[2] * kt.shape[3] == width
        stack_spec = pl.BlockSpec((1, 1) + kt.shape[2:4] + (tm,),
                                  lambda i: (layer, i // tiles, 0, 0, i % tiles))
        in_specs += [pl.BlockSpec(memory_space=pl.ANY)] * 2
        args += [kt, vt]
        out_specs += [stack_spec, stack_spec]
        out_shape += [jax.ShapeDtypeStruct(kt.shape, F32)] * 2
        aliases = {3: n_out, 4: n_out + 1}
    return pl.pallas_call(
        functools.partial(_inproj_body, n_out=n_out, heads_t=stacks is not None, scale0=scale0),
        grid=(m // tm,),
        in_specs=in_specs,
        out_specs=out_specs,
        out_shape=out_shape,
        input_output_aliases=aliases,
        compiler_params=_params(("parallel",), 56),
        name="inproj",
    )(*args)


def _attn_body(q_ref, k_ref, v_ref, o_ref, unfinished_ref, kb_ref, vb_ref, acc_ref, z_ref, ls_ref,
               after_ref, rs_ref, *, tq, tk, nq, q_off, group):
    if k_ref.dtype == BF16:
        kb_ref, vb_ref = k_ref.at[0], v_ref.at[0]
    else:
        kb_ref[...] = k_ref[0].astype(BF16)
        vb_ref[...] = v_ref[0].astype(BF16)
    rows = 2 * tq
    first_head = lax.broadcasted_iota(jnp.int32, (1, LANES), 1) < HEAD_DIM
    strict = (lax.broadcasted_iota(jnp.int32, (tk, tk), 0)
              > lax.broadcasted_iota(jnp.int32, (tk, tk), 1)).astype(BF16)
    qrow = lax.broadcasted_iota(jnp.int32, (rows, tk), 0) & (tq - 1)
    col = lax.broadcasted_iota(jnp.int32, (rows, tk), 1)
    sign_bit = jnp.uint32(0x80000000)

    def key_rows(j):
        return pl.ds(pl.multiple_of(jnp.maximum(j, 0) * tk, tk), tk)

    def logits(qs, j):
        return lax.dot_general(qs, kb_ref[key_rows(j), :], (((1,), (1,)), ((), ())),
                               preferred_element_type=F32)

    def log_gates(z, mask):
        if mask is not None:
            z = jnp.where(mask, z, MINUS_INF_LOG2)
        neg_abs = lax.bitcast_convert_type(lax.bitcast_convert_type(z, jnp.uint32) | sign_bit, F32)
        ls = jnp.minimum(z, 0.0) - jnp.log2(1.0 + jnp.exp2(neg_abs))
        lr = ls - z
        return ls, lr

    def suffix(lr):
        return jnp.dot(lr.astype(BF16), strict, preferred_element_type=F32)

    def weighted(w, j):
        return jnp.dot(w.astype(BF16), vb_ref[key_rows(j), :], preferred_element_type=F32)

    def middle():
        ls, lr = log_gates(z_ref[...], None)
        ls_ref[...] = ls
        after_ref[...] = suffix(lr)
        rs_ref[...] = jnp.sum(lr, axis=1, keepdims=True)

    def finish(j, carry):
        pv = weighted(jnp.exp2(ls_ref[...] + after_ref[...] + carry), j)
        return pv, carry + rs_ref[...]

    unfinished_ref[...] = jnp.zeros(unfinished_ref.shape, F32)

    def mark_unfinished(flag):
        unfinished_ref[...] = jnp.maximum(unfinished_ref[...], jnp.where(flag, 1.0, 0.0))

    def q_group(g, _):
        tiles = []
        for n in range(group):
            i = g * group + n
            r0 = pl.multiple_of(i * tq, tq)
            q = q_ref[0, pl.ds(r0, tq), :]
            zero = jnp.zeros_like(q)
            qs = jnp.concatenate([jnp.where(first_head, q, zero), jnp.where(first_head, zero, q)],
                                 axis=0)
            t_first = q_off + i * tq
            jmax = (t_first + tq - 1) // tk
            mask = (jmax * tk + col) < (t_first + qrow)
            tiles.append(dict(r0=r0, qs=qs, jmax=jmax, mask=mask,
                              z0=logits(qs, jmax), z1=logits(qs, jmax - 1)))
        for tl in tiles:
            tl["ls0"], lr0 = log_gates(tl.pop("z0"), tl["mask"])
            tl["after0"] = suffix(lr0)
            tl["rs0"] = jnp.sum(lr0, axis=1, keepdims=True)
            tl["ls1"], lr1 = log_gates(tl.pop("z1"), None)
            tl["after1"] = suffix(lr1)
            tl["rs1"] = jnp.sum(lr1, axis=1, keepdims=True)
        for n, tl in enumerate(tiles):
            jmax = tl["jmax"]
            w0 = jnp.exp2(tl["ls0"] + tl["after0"])
            no_tile1 = jnp.where(jmax >= 1, 0.0, MINUS_INF_LOG2)
            w1 = jnp.exp2(tl["ls1"] + tl["after1"] + (tl["rs0"] + no_tile1))
            acc_ref[n] = weighted(w0, jmax) + weighted(w1, jmax - 1)
            tl["carry"] = tl["rs0"] + jnp.where(jmax >= 1, tl["rs1"], 0.0)

        for n, tl in enumerate(tiles):
            qs, jmax, carry = tl["qs"], tl["jmax"], tl["carry"]
            alive = jnp.max(carry) > DEAD_LOG2
            mark_unfinished(jnp.logical_and(jmax < 2, alive))

            @pl.when(jnp.logical_and(jmax >= 2, alive))
            def _(n=n, qs=qs, jmax=jmax, carry=carry):
                z_ref[...] = logits(qs, jmax - 2)
                z_next = logits(qs, jmax - 3)
                middle()
                z_ref[...] = z_next

                def kstep(st):
                    t, carry, _ = st
                    z_next = logits(qs, jmax - t)
                    pv, carry = finish(jmax - t + 2, carry)
                    acc_ref[n] += pv
                    middle()
                    z_ref[...] = z_next
                    return t + 1, carry, jnp.max(carry) > DEAD_LOG2

                st = lax.while_loop(lambda st: jnp.logical_and(st[0] <= jmax + 2, st[2]), kstep,
                                    (jnp.int32(4), carry, jnp.bool_(True)))
                mark_unfinished(st[2])

            o_ref[0, pl.ds(tl["r0"], tq), :] = jnp.where(first_head, acc_ref[n, :tq, :],
                                                         acc_ref[n, tq:, :]).astype(o_ref.dtype)
        return 0

    lax.fori_loop(0, nq // group, q_group, 0)


def _attention(q, k_all, v_all, q_off):
    b, t, c = q.shape
    s = k_all.shape[1]
    tq = min(ATT_TQ, t)
    assert t % tq == 0 and ATT_TK % tq == 0 and q_off % tq == 0 and s % ATT_TK == 0
    nq = t // tq
    group = ATT_GROUP if nq % ATT_GROUP == 0 else 1
    body = functools.partial(_attn_body, tq=tq, tk=ATT_TK, nq=nq, q_off=q_off, group=group)
    qspec = pl.BlockSpec((1, t, LANES), lambda bi, hp: (bi, 0, hp))
    kspec = pl.BlockSpec((1, s, LANES), lambda bi, hp: (bi, 0, hp))
    return pl.pallas_call(
        body,
        grid=(b, c // LANES),
        in_specs=[qspec, kspec, kspec],
        out_specs=[qspec, pl.BlockSpec((1, 1, 8, LANES), lambda bi, hp: (bi, hp, 0, 0))],
        out_shape=[jax.ShapeDtypeStruct((b, t, c), BF16),
                   jax.ShapeDtypeStruct((b, c // LANES, 8, LANES), F32)],
        scratch_shapes=[pltpu.VMEM((s, LANES), BF16), pltpu.VMEM((s, LANES), BF16),
                        pltpu.VMEM((group, 2 * tq, LANES), F32),
                        pltpu.VMEM((2 * tq, ATT_TK), F32),
                        pltpu.VMEM((2 * tq, ATT_TK), F32),
                        pltpu.VMEM((2 * tq, ATT_TK), F32),
                        pltpu.VMEM((2 * tq, 1), F32)],
        compiler_params=_params(("parallel", "parallel"), 48),
        name="sb_attention",
    )(q, k_all, v_all)


def _ssm_body(u_ref, h0r_ref, h0i_ref, bd_ref, cm_ref, a_ref, d_ref, wg_ref, bg_ref,
              y_ref, hr_ref, hi_ref, h_sc, utb_sc, bu_sc, ytb_sc, *, chunk, batch, ntile):
    half = bd_ref.shape[2] // 2

    @pl.when(pl.program_id(0) == 0)
    def _():
        for j in range(ntile):
            h_sc[j, 0] = h0r_ref[:, j * half:(j + 1) * half]
            h_sc[j, 1] = h0i_ref[:, j * half:(j + 1) * half]

    for bi in range(batch):
        for j in range(ntile):
            utb_sc[j, pl.ds(bi, chunk, stride=batch), :] = (
                u_ref[bi, :, j * LANES:(j + 1) * LANES].astype(F32))

    for j in range(ntile):
        bu_sc[...] = jnp.dot(utb_sc[j].astype(BF16), bd_ref[j], preferred_element_type=F32)
        ar = a_ref[j, 0:1, :]
        ai = a_ref[j, 1:2, :]

        def step(t, h):
            hr, hi = h
            r0 = pl.multiple_of(t * batch, batch)
            nr = ar * hr - ai * hi + bu_sc[pl.ds(r0, batch), :half]
            ni = ar * hi + ai * hr + bu_sc[pl.ds(r0, batch), half:]
            bu_sc[pl.ds(r0, batch), :half] = nr
            bu_sc[pl.ds(r0, batch), half:] = ni
            return nr, ni

        hr, hi = lax.fori_loop(0, chunk, step, (h_sc[j, 0], h_sc[j, 1]), unroll=True)
        h_sc[j, 0] = hr
        h_sc[j, 1] = hi
        hr_ref[:, j * half:(j + 1) * half] = hr
        hi_ref[:, j * half:(j + 1) * half] = hi
        ytb_sc[j] = jnp.dot(bu_sc[...].astype(BF16), cm_ref[j], preferred_element_type=F32)

    ch = jnp.concatenate([ytb_sc[j] for j in range(ntile)], axis=1)
    uu = jnp.concatenate([utb_sc[j] for j in range(ntile)], axis=1)
    y = jax.nn.gelu(ch + d_ref[...] * uu)
    gl = jnp.dot(y.astype(BF16), wg_ref[...], preferred_element_type=F32) + bg_ref[...]
    out = y * jax.nn.sigmoid(gl)
    for j in range(ntile):
        ytb_sc[j] = out[:, j * LANES:(j + 1) * LANES]
    for bi in range(batch):
        for j in range(ntile):
            y_ref[bi, :, j * LANES:(j + 1) * LANES] = (
                ytb_sc[j, pl.ds(bi, chunk, stride=batch), :].astype(y_ref.dtype))


def _ssm(u, h0r, h0i, bd, cm, a, d, wg_bf, bg):
    b, t, c = u.shape
    ntile = c // LANES
    states = h0r.shape[1]
    chunk = min(SSM_CHUNK, t)
    body = functools.partial(_ssm_body, chunk=chunk, batch=b, ntile=ntile)
    const = lambda shape: pl.BlockSpec(shape, lambda i: (0,) * len(shape))
    return pl.pallas_call(
        body,
        grid=(t // chunk,),
        in_specs=[pl.BlockSpec((b, chunk, c), lambda i: (0, i, 0)),
                  const((b, states)), const((b, states)),
                  const(bd.shape), const(cm.shape), const(a.shape),
                  const((1, c)), const(wg_bf.shape), const((1, c))],
        out_specs=[pl.BlockSpec((b, chunk, c), lambda i: (0, i, 0)),
                   const((b, states)), const((b, states))],
        out_shape=[jax.ShapeDtypeStruct((b, t, c), BF16),
                   jax.ShapeDtypeStruct((b, states), F32),
                   jax.ShapeDtypeStruct((b, states), F32)],
        scratch_shapes=[pltpu.VMEM((ntile, 2, b, states // ntile), F32),
                        pltpu.VMEM((ntile, chunk * b, LANES), F32),
                        pltpu.VMEM((chunk * b, 2 * states // ntile), F32),
                        pltpu.VMEM((ntile, chunk * b, LANES), F32)],
        compiler_params=_params(("arbitrary",), 48),
        name="s5_branch",
    )(u, h0r, h0i, bd, cm, a, d.reshape(1, c), wg_bf, bg.reshape(1, c))


def _ssm_weights(a_re, a_im, log_dt, b_re, b_im, c_re, c_im, ntile):
    g, p = a_re.shape
    gt = g // ntile
    dt = jnp.exp(log_dt)[:, None]
    mag = jnp.exp(a_re * dt)
    ang = a_im * dt
    ab_re = mag * jnp.cos(ang)
    ab_im = mag * jnp.sin(ang)
    den = a_re * a_re + a_im * a_im
    f_re = ((ab_re - 1.0) * a_re + ab_im * a_im) / den
    f_im = (ab_im * a_re - (ab_re - 1.0) * a_im) / den
    bb_re = f_re[..., None] * b_re - f_im[..., None] * b_im
    bb_im = f_re[..., None] * b_im + f_im[..., None] * b_re
    eye = jnp.eye(gt, dtype=F32)

    def expand(bb):
        bb = bb.reshape(ntile, gt, p, SSM_GROUP)
        return jnp.einsum("jgpc,gh->jgchp", bb, eye).reshape(ntile, gt * SSM_GROUP, gt * p)

    def contract(cc):
        cc = cc.reshape(ntile, gt, SSM_GROUP, p)
        return jnp.einsum("jgcp,gh->jhpgc", cc, eye).reshape(ntile, gt * p, gt * SSM_GROUP)

    bd = jnp.concatenate([expand(bb_re), expand(bb_im)], axis=2).astype(BF16)
    cm = jnp.concatenate([contract(c_re), -contract(c_im)], axis=1).astype(BF16)
    a = jnp.stack([ab_re.reshape(ntile, gt * p), ab_im.reshape(ntile, gt * p)], axis=1)
    return bd, cm, a


def _outproj_body(att_ref, ga_ref, sy_ref, gs_ref, x_ref, gatt_ref, gssm_ref, w_ref, gout_ref,
                  o_ref, *, norm_out):
    ga = ga_ref[...].astype(F32)
    gs = gs_ref[...].astype(F32)
    a = (_rms(att_ref[...].astype(F32), gatt_ref[...]) * (ga * jax.nn.sigmoid(ga))).astype(BF16)
    s = (_rms(sy_ref[...].astype(F32), gssm_ref[...]) * (gs * jax.nn.sigmoid(gs))).astype(BF16)
    d_att = a.shape[1]
    out = (jnp.dot(a, w_ref[:d_att, :], preferred_element_type=F32)
           + jnp.dot(s, w_ref[d_att:, :], preferred_element_type=F32))
    x_new = x_ref[...] + out
    o_ref[...] = _rms(x_new, gout_ref[...]) if norm_out else x_new


def _outproj(att, ga, sy, gs, x2d, g_att, g_ssm, w_bf, g_out, norm_out):
    m, d = x2d.shape
    c = att.shape[1]
    tm = min(ROW_TILE, m)
    half = pl.BlockSpec((tm, c), lambda i: (i, 0))
    return pl.pallas_call(
        functools.partial(_outproj_body, norm_out=norm_out),
        grid=(m // tm,),
        in_specs=[half, half, half, half,
                  pl.BlockSpec((tm, d), lambda i: (i, 0)),
                  pl.BlockSpec((1, c), lambda i: (0, 0)),
                  pl.BlockSpec((1, c), lambda i: (0, 0)),
                  pl.BlockSpec(w_bf.shape, lambda i: (0, 0)),
                  pl.BlockSpec((1, d), lambda i: (0, 0))],
        out_specs=pl.BlockSpec((tm, d), lambda i: (i, 0)),
        out_shape=jax.ShapeDtypeStruct((m, d), F32),
        compiler_params=_params(("parallel",), 48),
        name="outproj",
    )(att, ga, sy, gs, x2d, g_att.reshape(1, c), g_ssm.reshape(1, c), w_bf, g_out.reshape(1, d))


def _cache_rows_body(x_ref, o_ref):
    for h in range(x_ref.shape[3]):
        o_ref[0, :, h * HEAD_DIM:(h + 1) * HEAD_DIM] = x_ref[0, 0, :, h, :]


def _cache_rows(cache, layer):
    _, b, p, h, dh = cache.shape
    rows = min(ATT_TK, p)
    assert p % rows == 0 and dh == HEAD_DIM
    return pl.pallas_call(
        _cache_rows_body,
        grid=(b, p // rows),
        in_specs=[pl.BlockSpec((1, 1, rows, h, dh), lambda bi, j: (layer, bi, j, 0, 0))],
        out_specs=pl.BlockSpec((1, rows, h * dh), lambda bi, j: (bi, j, 0)),
        out_shape=jax.ShapeDtypeStruct((b, p, h * dh), F32),
        compiler_params=_params(("parallel", "parallel"), 32),
        name="cache_rows",
    )(cache)


def _sample_attention(q3, k3, v3, cache_k, cache_v, layer):
    b, t, c = q3.shape
    p = cache_k.shape[2]

    def with_past(k_rows, v_rows):
        n = k_rows.shape[1]
        zeros = jnp.zeros((b, (-(n + t)) % ATT_TK, c), F32)
        return _attention(q3, jnp.concatenate([k_rows, k3, zeros], axis=1),
                          jnp.concatenate([v_rows, v3, zeros], axis=1), n)

    def whole_past():
        return with_past(_cache_rows(cache_k, layer), _cache_rows(cache_v, layer))[0]

    if p <= ATT_TK:
        return whole_past()
    att, unfinished = with_past(cache_k[layer, :, p - ATT_TK:].reshape(b, ATT_TK, c),
                                cache_v[layer, :, p - ATT_TK:].reshape(b, ATT_TK, c))
    return lax.cond(jnp.max(unfinished) > 0.0, whole_past, lambda: att)


def _layer(x, past, stacks, h0_re, h0_im, ln_g, w_in_bf, ssm_w, d_skip, w_glu_bf, b_glu,
           g_att, g_ssm, w_out_bf, g_out, norm_out):
    b, t, d = x.shape
    x2d = x.reshape(b * t, d)
    if stacks is None:
        q, k, v, ga, u, gs = _inproj(x2d, ln_g, w_in_bf, (BF16, F32, F32, BF16, BF16, BF16), ATT_Q_SCALE)
    else:
        q, k, v, ga, u, gs, kt, vt = _inproj(x2d, ln_g, w_in_bf, (BF16,) * 6, ATT_Q_SCALE, stacks + (t,))
    c = q.shape[1]
    q3, k3, v3 = (z.reshape(b, t, c) for z in (q, k, v))
    if past is None:
        att, _ = _attention(q3, k3, v3, 0)
    else:
        att = _sample_attention(q3, k3, v3, *past)
    bd, cm, a = ssm_w
    states = a.shape[0] * a.shape[2]
    if h0_re is None:
        h0_re = jnp.zeros((b, states), F32)
        h0_im = jnp.zeros((b, states), F32)
    sy, h_re, h_im = _ssm(u.reshape(b, t, c), h0_re.reshape(b, states), h0_im.reshape(b, states),
                          bd, cm, a, d_skip, w_glu_bf, b_glu)
    x_new = _outproj(att.reshape(b * t, c), ga, sy.reshape(b * t, c), gs, x2d, g_att, g_ssm, w_out_bf,
                     g_out, norm_out)
    return x_new.reshape(b, t, d), ((k3, v3) if stacks is None else (kt, vt)), h_re, h_im


def kernel(x_prompt, x_sample, cache_k, cache_v, state_ssm_re, state_ssm_im, ln_g, w_in, ssm_a_re, ssm_a_im, ssm_log_dt, ssm_b_re, ssm_b_im, ssm_c_re, ssm_c_im, ssm_d, w_glu, b_glu, g_att, g_ssm, w_out, final_g):
    depth = w_in.shape[0]
    n_groups, n_state = ssm_a_re.shape[1:]
    c = ssm_d.shape[1]
    n_heads = c // HEAD_DIM
    xp, xs = x_prompt, x_sample
    bp, tp, _ = xp.shape
    kt = jnp.zeros((depth, bp, n_heads, HEAD_DIM, tp), F32)
    vt = jnp.zeros((depth, bp, n_heads, HEAD_DIM, tp), F32)
    hrp, hip, ksl, vsl, hrs, his = ([] for _ in range(6))
    for l in range(depth):
        ssm_w = _ssm_weights(ssm_a_re[l], ssm_a_im[l], ssm_log_dt[l], ssm_b_re[l], ssm_b_im[l],
                             ssm_c_re[l], ssm_c_im[l], c // LANES)
        w = (ln_g[l], w_in[l].astype(BF16), ssm_w, ssm_d[l], w_glu[l].astype(BF16), b_glu[l],
             g_att[l], g_ssm[l], w_out[l].astype(BF16), final_g, l == depth - 1)
        xp, (kt, vt), r1, i1 = _layer(xp, None, (kt, vt, l), None, None, *w)
        xs, (k2, v2), r2, i2 = _layer(xs, (cache_k, cache_v, l), None, state_ssm_re[l], state_ssm_im[l], *w)
        for lst, val in zip((hrp, hip, ksl, vsl, hrs, his), (r1, i1, k2, v2, r2, i2)):
            lst.append(val)

    def time_major(zt):
        return zt.transpose(0, 1, 4, 2, 3)

    def heads(zs):
        z = jnp.stack(zs)
        return z.reshape(z.shape[:3] + (n_heads, HEAD_DIM))

    def states(zs):
        z = jnp.stack(zs)
        return z.reshape(z.shape[:2] + (n_groups, n_state))

    return (xp, xs, time_major(kt), time_major(vt), states(hrp), states(hip),
            heads(ksl), heads(vsl), states(hrs), states(his))
```

```python
import functools
import math

import jax
import jax.numpy as jnp
from jax import lax
from jax.experimental import pallas as pl
from jax.experimental.pallas import tpu as pltpu

F32 = jnp.float32
BF16 = jnp.bfloat16

EPS = 1e-6
HEAD_DIM = 64
SSM_GROUP = 16
LANES = 128
ATT_TK = 256
ATT_TQ = 256
ATT_GROUP = 2
DEAD_LOG2 = -160.0
ATT_Q_SCALE = math.log2(math.e) / math.sqrt(HEAD_DIM)
MINUS_INF_LOG2 = -1e30
SSM_CHUNK = 64
ROW_TILE = 512
MIB = 1024 * 1024


def _params(sem, vmem_mib):
    return pltpu.CompilerParams(dimension_semantics=sem, vmem_limit_bytes=vmem_mib * MIB)


def _rms(x, g):
    ms = jnp.mean(x * x, axis=-1, keepdims=True)
    return x * lax.rsqrt(ms + EPS) * g


def _inproj_body(x_ref, g_ref, w_ref, *refs, n_out, n_alias, heads_t, scale0):
    hn = _rms(x_ref[...], g_ref[...]).astype(BF16)
    out_refs = refs[n_alias:n_alias + n_out]
    width = out_refs[0].shape[-1]
    for n, o_ref in enumerate(out_refs):
        y = jnp.dot(hn, w_ref[:, n * width:(n + 1) * width], preferred_element_type=F32)
        o_ref[...] = (y * scale0 if n == 0 else y).astype(o_ref.dtype)
        if heads_t and n in (1, 2):
            t_ref = refs[n_alias + n_out + n - 1]
            t_ref[0, 0] = y.T.reshape(t_ref.shape[2:])


def _inproj(x2d, g, w_bf, out_dtypes, scale0, stacks=None):
    m, d = x2d.shape
    n_out = len(out_dtypes)
    width = w_bf.shape[1] // n_out
    tm = min(ROW_TILE, m)
    in_specs = [pl.BlockSpec((tm, d), lambda i: (i, 0)),
                pl.BlockSpec((1, d), lambda i: (0, 0)),
                pl.BlockSpec(w_bf.shape, lambda i: (0, 0))]
    out_specs = [pl.BlockSpec((tm, width), lambda i: (i, 0))] * n_out
    out_shape = [jax.ShapeDtypeStruct((m, width), dt) for dt in out_dtypes]
    args = [x2d, g.reshape(1, d), w_bf]
    aliases = {}
    if stacks is not None:
        kt, vt, layer, seq, shape = stacks
        tiles = seq // tm
        assert seq % tm == 0 and shape[2] * shape[3] == width
        stack_spec = pl.BlockSpec((1, 1) + shape[2:4] + (tm,),
                                  lambda i: (layer, i // tiles, 0, 0, i % tiles))
        out_specs += [stack_spec, stack_spec]
        out_shape += [jax.ShapeDtypeStruct(shape, F32)] * 2
        if kt is not None:
            in_specs += [pl.BlockSpec(memory_space=pl.ANY)] * 2
            args += [kt, vt]
            aliases = {3: n_out, 4: n_out + 1}
    return pl.pallas_call(
        functools.partial(_inproj_body, n_out=n_out, n_alias=len(aliases),
                          heads_t=stacks is not None, scale0=scale0),
        grid=(m // tm,),
        in_specs=in_specs,
        out_specs=out_specs,
        out_shape=out_shape,
        input_output_aliases=aliases,
        compiler_params=_params(("parallel",), 56),
        name="inproj",
    )(*args)


def _attn_body(q_ref, k_ref, v_ref, o_ref, unfinished_ref, kb_ref, vb_ref, acc_ref, z_ref, ls_ref,
               after_ref, rs_ref, *, tq, tk, nq, q_off, group):
    if k_ref.dtype == BF16:
        kb_ref, vb_ref = k_ref.at[0], v_ref.at[0]
    else:
        kb_ref[...] = k_ref[0].astype(BF16)
        vb_ref[...] = v_ref[0].astype(BF16)
    rows = 2 * tq
    first_head = lax.broadcasted_iota(jnp.int32, (1, LANES), 1) < HEAD_DIM
    strict = (lax.broadcasted_iota(jnp.int32, (tk, tk), 0)
              > lax.broadcasted_iota(jnp.int32, (tk, tk), 1)).astype(BF16)
    qrow = lax.broadcasted_iota(jnp.int32, (rows, tk), 0) & (tq - 1)
    col = lax.broadcasted_iota(jnp.int32, (rows, tk), 1)
    sign_bit = jnp.uint32(0x80000000)

    def key_rows(j):
        return pl.ds(pl.multiple_of(jnp.maximum(j, 0) * tk, tk), tk)

    def logits(qs, j):
        return lax.dot_general(qs, kb_ref[key_rows(j), :], (((1,), (1,)), ((), ())),
                               preferred_element_type=F32)

    def log_gates(z, mask):
        if mask is not None:
            z = jnp.where(mask, z, MINUS_INF_LOG2)
        neg_abs = lax.bitcast_convert_type(lax.bitcast_convert_type(z, jnp.uint32) | sign_bit, F32)
        ls = jnp.minimum(z, 0.0) - jnp.log2(1.0 + jnp.exp2(neg_abs))
        lr = ls - z
        return ls, lr

    def suffix(lr):
        return jnp.dot(lr.astype(BF16), strict, preferred_element_type=F32)

    def weighted(w, j):
        return jnp.dot(w.astype(BF16), vb_ref[key_rows(j), :], preferred_element_type=F32)

    def middle():
        ls, lr = log_gates(z_ref[...], None)
        ls_ref[...] = ls
        after_ref[...] = suffix(lr)
        rs_ref[...] = jnp.sum(lr, axis=1, keepdims=True)

    def finish(j, carry):
        pv = weighted(jnp.exp2(ls_ref[...] + after_ref[...] + carry), j)
        return pv, carry + rs_ref[...]

    unfinished_ref[...] = jnp.zeros(unfinished_ref.shape, F32)

    def mark_unfinished(flag):
        unfinished_ref[...] = jnp.maximum(unfinished_ref[...], jnp.where(flag, 1.0, 0.0))

    def q_group(g, _):
        tiles = []
        for n in range(group):
            i = g * group + n
            r0 = pl.multiple_of(i * tq, tq)
            q = q_ref[0, pl.ds(r0, tq), :]
            zero = jnp.zeros_like(q)
            qs = jnp.concatenate([jnp.where(first_head, q, zero), jnp.where(first_head, zero, q)],
                                 axis=0)
            t_first = q_off + i * tq
            jmax = (t_first + tq - 1) // tk
            mask = (jmax * tk + col) < (t_first + qrow)
            tiles.append(dict(r0=r0, qs=qs, jmax=jmax, mask=mask,
                              z0=logits(qs, jmax), z1=logits(qs, jmax - 1)))
        for tl in tiles:
            tl["ls0"], lr0 = log_gates(tl.pop("z0"), tl["mask"])
            tl["after0"] = suffix(lr0)
            tl["rs0"] = jnp.sum(lr0, axis=1, keepdims=True)
            tl["ls1"], lr1 = log_gates(tl.pop("z1"), None)
            tl["after1"] = suffix(lr1)
            tl["rs1"] = jnp.sum(lr1, axis=1, keepdims=True)
        for n, tl in enumerate(tiles):
            jmax = tl["jmax"]
            w0 = jnp.exp2(tl["ls0"] + tl["after0"])
            no_tile1 = jnp.where(jmax >= 1, 0.0, MINUS_INF_LOG2)
            w1 = jnp.exp2(tl["ls1"] + tl["after1"] + (tl["rs0"] + no_tile1))
            acc_ref[n] = weighted(w0, jmax) + weighted(w1, jmax - 1)
            tl["carry"] = tl["rs0"] + jnp.where(jmax >= 1, tl["rs1"], 0.0)

        for n, tl in enumerate(tiles):
            qs, jmax, carry = tl["qs"], tl["jmax"], tl["carry"]
            alive = jnp.max(carry) > DEAD_LOG2
            mark_unfinished(jnp.logical_and(jmax < 2, alive))

            @pl.when(jnp.logical_and(jmax >= 2, alive))
            def _(n=n, qs=qs, jmax=jmax, carry=carry):
                z_ref[...] = logits(qs, jmax - 2)
                z_next = logits(qs, jmax - 3)
                middle()
                z_ref[...] = z_next

                def kstep(st):
                    t, carry, _ = st
                    z_next = logits(qs, jmax - t)
                    pv, carry = finish(jmax - t + 2, carry)
                    acc_ref[n] += pv
                    middle()
                    z_ref[...] = z_next
                    return t + 1, carry, jnp.max(carry) > DEAD_LOG2

                st = lax.while_loop(lambda st: jnp.logical_and(st[0] <= jmax + 2, st[2]), kstep,
                                    (jnp.int32(4), carry, jnp.bool_(True)))
                mark_unfinished(st[2])

            o_ref[0, pl.ds(tl["r0"], tq), :] = jnp.where(first_head, acc_ref[n, :tq, :],
                                                         acc_ref[n, tq:, :]).astype(o_ref.dtype)
        return 0

    lax.fori_loop(0, nq // group, q_group, 0)


def _attention(q, k_all, v_all, q_off):
    b, t, c = q.shape
    s = k_all.shape[1]
    tq = min(ATT_TQ, t)
    assert t % tq == 0 and ATT_TK % tq == 0 and q_off % tq == 0 and s % ATT_TK == 0
    nq = t // tq
    group = ATT_GROUP if nq % ATT_GROUP == 0 else 1
    body = functools.partial(_attn_body, tq=tq, tk=ATT_TK, nq=nq, q_off=q_off, group=group)
    qspec = pl.BlockSpec((1, t, LANES), lambda bi, hp: (bi, 0, hp))
    kspec = pl.BlockSpec((1, s, LANES), lambda bi, hp: (bi, 0, hp))
    return pl.pallas_call(
        body,
        grid=(b, c // LANES),
        in_specs=[qspec, kspec, kspec],
        out_specs=[qspec, pl.BlockSpec((1, 1, 8, LANES), lambda bi, hp: (bi, hp, 0, 0))],
        out_shape=[jax.ShapeDtypeStruct((b, t, c), BF16),
                   jax.ShapeDtypeStruct((b, c // LANES, 8, LANES), F32)],
        scratch_shapes=[pltpu.VMEM((s, LANES), BF16), pltpu.VMEM((s, LANES), BF16),
                        pltpu.VMEM((group, 2 * tq, LANES), F32),
                        pltpu.VMEM((2 * tq, ATT_TK), F32),
                        pltpu.VMEM((2 * tq, ATT_TK), F32),
                        pltpu.VMEM((2 * tq, ATT_TK), F32),
                        pltpu.VMEM((2 * tq, 1), F32)],
        compiler_params=_params(("parallel", "parallel"), 48),
        name="sb_attention",
    )(q, k_all, v_all)


def _ssm_body(u_ref, h0r_ref, h0i_ref, bd_ref, cm_ref, a_ref, d_ref, wg_ref, bg_ref,
              y_ref, hr_ref, hi_ref, h_sc, utb_sc, bu_sc, ytb_sc, *, chunk, batch, ntile):
    half = bd_ref.shape[2] // 2

    @pl.when(pl.program_id(0) == 0)
    def _():
        for j in range(ntile):
            h_sc[j, 0] = h0r_ref[:, j * half:(j + 1) * half]
            h_sc[j, 1] = h0i_ref[:, j * half:(j + 1) * half]

    for bi in range(batch):
        for j in range(ntile):
            utb_sc[j, pl.ds(bi, chunk, stride=batch), :] = (
                u_ref[bi, :, j * LANES:(j + 1) * LANES].astype(F32))

    for j in range(ntile):
        bu_sc[...] = jnp.dot(utb_sc[j].astype(BF16), bd_ref[j], preferred_element_type=F32)
        ar = a_ref[j, 0:1, :]
        ai = a_ref[j, 1:2, :]

        def step(t, h):
            hr, hi = h
            r0 = pl.multiple_of(t * batch, batch)
            nr = ar * hr - ai * hi + bu_sc[pl.ds(r0, batch), :half]
            ni = ar * hi + ai * hr + bu_sc[pl.ds(r0, batch), half:]
            bu_sc[pl.ds(r0, batch), :half] = nr
            bu_sc[pl.ds(r0, batch), half:] = ni
            return nr, ni

        hr, hi = lax.fori_loop(0, chunk, step, (h_sc[j, 0], h_sc[j, 1]), unroll=True)
        h_sc[j, 0] = hr
        h_sc[j, 1] = hi
        hr_ref[:, j * half:(j + 1) * half] = hr
        hi_ref[:, j * half:(j + 1) * half] = hi
        ytb_sc[j] = jnp.dot(bu_sc[...].astype(BF16), cm_ref[j], preferred_element_type=F32)

    ch = jnp.concatenate([ytb_sc[j] for j in range(ntile)], axis=1)
    uu = jnp.concatenate([utb_sc[j] for j in range(ntile)], axis=1)
    y = jax.nn.gelu(ch + d_ref[...] * uu)
    gl = jnp.dot(y.astype(BF16), wg_ref[...], preferred_element_type=F32) + bg_ref[...]
    out = y * jax.nn.sigmoid(gl)
    for j in range(ntile):
        ytb_sc[j] = out[:, j * LANES:(j + 1) * LANES]
    for bi in range(batch):
        for j in range(ntile):
            y_ref[bi, :, j * LANES:(j + 1) * LANES] = (
                ytb_sc[j, pl.ds(bi, chunk, stride=batch), :].astype(y_ref.dtype))


def _ssm(u, h0r, h0i, bd, cm, a, d, wg_bf, bg):
    b, t, c = u.shape
    ntile = c // LANES
    states = h0r.shape[1]
    chunk = min(SSM_CHUNK, t)
    body = functools.partial(_ssm_body, chunk=chunk, batch=b, ntile=ntile)
    const = lambda shape: pl.BlockSpec(shape, lambda i: (0,) * len(shape))
    return pl.pallas_call(
        body,
        grid=(t // chunk,),
        in_specs=[pl.BlockSpec((b, chunk, c), lambda i: (0, i, 0)),
                  const((b, states)), const((b, states)),
                  const(bd.shape), const(cm.shape), const(a.shape),
                  const((1, c)), const(wg_bf.shape), const((1, c))],
        out_specs=[pl.BlockSpec((b, chunk, c), lambda i: (0, i, 0)),
                   const((b, states)), const((b, states))],
        out_shape=[jax.ShapeDtypeStruct((b, t, c), BF16),
                   jax.ShapeDtypeStruct((b, states), F32),
                   jax.ShapeDtypeStruct((b, states), F32)],
        scratch_shapes=[pltpu.VMEM((ntile, 2, b, states // ntile), F32),
                        pltpu.VMEM((ntile, chunk * b, LANES), F32),
                        pltpu.VMEM((chunk * b, 2 * states // ntile), F32),
                        pltpu.VMEM((ntile, chunk * b, LANES), F32)],
        compiler_params=_params(("arbitrary",), 48),
        name="s5_branch",
    )(u, h0r, h0i, bd, cm, a, d.reshape(1, c), wg_bf, bg.reshape(1, c))


def _ssm_weights(a_re, a_im, log_dt, b_re, b_im, c_re, c_im, ntile):
    g, p = a_re.shape
    gt = g // ntile
    dt = jnp.exp(log_dt)[:, None]
    mag = jnp.exp(a_re * dt)
    ang = a_im * dt
    ab_re = mag * jnp.cos(ang)
    ab_im = mag * jnp.sin(ang)
    den = a_re * a_re + a_im * a_im
    f_re = ((ab_re - 1.0) * a_re + ab_im * a_im) / den
    f_im = (ab_im * a_re - (ab_re - 1.0) * a_im) / den
    bb_re = f_re[..., None] * b_re - f_im[..., None] * b_im
    bb_im = f_re[..., None] * b_im + f_im[..., None] * b_re
    eye = jnp.eye(gt, dtype=F32)

    def expand(bb):
        bb = bb.reshape(ntile, gt, p, SSM_GROUP)
        return jnp.einsum("jgpc,gh->jgchp", bb, eye).reshape(ntile, gt * SSM_GROUP, gt * p)

    def contract(cc):
        cc = cc.reshape(ntile, gt, SSM_GROUP, p)
        return jnp.einsum("jgcp,gh->jhpgc", cc, eye).reshape(ntile, gt * p, gt * SSM_GROUP)

    bd = jnp.concatenate([expand(bb_re), expand(bb_im)], axis=2).astype(BF16)
    cm = jnp.concatenate([contract(c_re), -contract(c_im)], axis=1).astype(BF16)
    a = jnp.stack([ab_re.reshape(ntile, gt * p), ab_im.reshape(ntile, gt * p)], axis=1)
    return bd, cm, a


def _outproj_body(att_ref, ga_ref, sy_ref, gs_ref, x_ref, gatt_ref, gssm_ref, w_ref, gout_ref,
                  o_ref, *, norm_out):
    ga = ga_ref[...].astype(F32)
    gs = gs_ref[...].astype(F32)
    a = (_rms(att_ref[...].astype(F32), gatt_ref[...]) * (ga * jax.nn.sigmoid(ga))).astype(BF16)
    s = (_rms(sy_ref[...].astype(F32), gssm_ref[...]) * (gs * jax.nn.sigmoid(gs))).astype(BF16)
    d_att = a.shape[1]
    out = (jnp.dot(a, w_ref[:d_att, :], preferred_element_type=F32)
           + jnp.dot(s, w_ref[d_att:, :], preferred_element_type=F32))
    x_new = x_ref[...] + out
    o_ref[...] = _rms(x_new, gout_ref[...]) if norm_out else x_new


def _outproj(att, ga, sy, gs, x2d, g_att, g_ssm, w_bf, g_out, norm_out):
    m, d = x2d.shape
    c = att.shape[1]
    tm = min(ROW_TILE, m)
    half = pl.BlockSpec((tm, c), lambda i: (i, 0))
    return pl.pallas_call(
        functools.partial(_outproj_body, norm_out=norm_out),
        grid=(m // tm,),
        in_specs=[half, half, half, half,
                  pl.BlockSpec((tm, d), lambda i: (i, 0)),
                  pl.BlockSpec((1, c), lambda i: (0, 0)),
                  pl.BlockSpec((1, c), lambda i: (0, 0)),
                  pl.BlockSpec(w_bf.shape, lambda i: (0, 0)),
                  pl.BlockSpec((1, d), lambda i: (0, 0))],
        out_specs=pl.BlockSpec((tm, d), lambda i: (i, 0)),
        out_shape=jax.ShapeDtypeStruct((m, d), F32),
        compiler_params=_params(("parallel",), 48),
        name="outproj",
    )(att, ga, sy, gs, x2d, g_att.reshape(1, c), g_ssm.reshape(1, c), w_bf, g_out.reshape(1, d))


def _cache_rows_body(x_ref, o_ref):
    for h in range(x_ref.shape[3]):
        o_ref[0, :, h * HEAD_DIM:(h + 1) * HEAD_DIM] = x_ref[0, 0, :, h, :]


def _cache_rows(cache, layer):
    _, b, p, h, dh = cache.shape
    rows = min(ATT_TK, p)
    assert p % rows == 0 and dh == HEAD_DIM
    return pl.pallas_call(
        _cache_rows_body,
        grid=(b, p // rows),
        in_specs=[pl.BlockSpec((1, 1, rows, h, dh), lambda bi, j: (layer, bi, j, 0, 0))],
        out_specs=pl.BlockSpec((1, rows, h * dh), lambda bi, j: (bi, j, 0)),
        out_shape=jax.ShapeDtypeStruct((b, p, h * dh), F32),
        compiler_params=_params(("parallel", "parallel"), 32),
        name="cache_rows",
    )(cache)


def _sample_attention(q3, k3, v3, cache_k, cache_v, layer):
    b, t, c = q3.shape
    p = cache_k.shape[2]

    def with_past(k_rows, v_rows):
        n = k_rows.shape[1]
        zeros = jnp.zeros((b, (-(n + t)) % ATT_TK, c), F32)
        return _attention(q3, jnp.concatenate([k_rows, k3, zeros], axis=1),
                          jnp.concatenate([v_rows, v3, zeros], axis=1), n)

    def whole_past():
        return with_past(_cache_rows(cache_k, layer), _cache_rows(cache_v, layer))[0]

    if p <= ATT_TK:
        return whole_past()
    att, unfinished = with_past(cache_k[layer, :, p - ATT_TK:].reshape(b, ATT_TK, c),
                                cache_v[layer, :, p - ATT_TK:].reshape(b, ATT_TK, c))
    return lax.cond(jnp.max(unfinished) > 0.0, whole_past, lambda: att)


def _layer(x, past, stacks, h0_re, h0_im, ln_g, w_in_bf, ssm_w, d_skip, w_glu_bf, b_glu,
           g_att, g_ssm, w_out_bf, g_out, norm_out):
    b, t, d = x.shape
    x2d = x.reshape(b * t, d)
    if stacks is None:
        q, k, v, ga, u, gs = _inproj(x2d, ln_g, w_in_bf, (BF16, F32, F32, BF16, BF16, BF16), ATT_Q_SCALE)
    else:
        stack_shape = (stacks[3], b, w_in_bf.shape[1] // 6 // HEAD_DIM, HEAD_DIM, t)
        q, k, v, ga, u, gs, kt, vt = _inproj(x2d, ln_g, w_in_bf, (BF16,) * 6, ATT_Q_SCALE,
                                             stacks[:3] + (t, stack_shape))
    c = q.shape[1]
    q3, k3, v3 = (z.reshape(b, t, c) for z in (q, k, v))
    if past is None:
        att, _ = _attention(q3, k3, v3, 0)
    else:
        att = _sample_attention(q3, k3, v3, *past)
    bd, cm, a = ssm_w
    states = a.shape[0] * a.shape[2]
    if h0_re is None:
        h0_re = jnp.zeros((b, states), F32)
        h0_im = jnp.zeros((b, states), F32)
    sy, h_re, h_im = _ssm(u.reshape(b, t, c), h0_re.reshape(b, states), h0_im.reshape(b, states),
                          bd, cm, a, d_skip, w_glu_bf, b_glu)
    x_new = _outproj(att.reshape(b * t, c), ga, sy.reshape(b * t, c), gs, x2d, g_att, g_ssm, w_out_bf,
                     g_out, norm_out)
    return x_new.reshape(b, t, d), ((k3, v3) if stacks is None else (kt, vt)), h_re, h_im


def kernel(x_prompt, x_sample, cache_k, cache_v, state_ssm_re, state_ssm_im, ln_g, w_in, ssm_a_re, ssm_a_im, ssm_log_dt, ssm_b_re, ssm_b_im, ssm_c_re, ssm_c_im, ssm_d, w_glu, b_glu, g_att, g_ssm, w_out, final_g):
    depth = w_in.shape[0]
    n_groups, n_state = ssm_a_re.shape[1:]
    c = ssm_d.shape[1]
    n_heads = c // HEAD_DIM
    xp, xs = x_prompt, x_sample
    kt = vt = None
    hrp, hip, ksl, vsl, hrs, his = ([] for _ in range(6))
    for l in range(depth):
        ssm_w = _ssm_weights(ssm_a_re[l], ssm_a_im[l], ssm_log_dt[l], ssm_b_re[l], ssm_b_im[l],
                             ssm_c_re[l], ssm_c_im[l], c // LANES)
        w = (ln_g[l], w_in[l].astype(BF16), ssm_w, ssm_d[l], w_glu[l].astype(BF16), b_glu[l],
             g_att[l], g_ssm[l], w_out[l].astype(BF16), final_g, l == depth - 1)
        xp, (kt, vt), r1, i1 = _layer(xp, None, (kt, vt, l, depth), None, None, *w)
        xs, (k2, v2), r2, i2 = _layer(xs, (cache_k, cache_v, l), None, state_ssm_re[l], state_ssm_im[l], *w)
        for lst, val in zip((hrp, hip, ksl, vsl, hrs, his), (r1, i1, k2, v2, r2, i2)):
            lst.append(val)

    def time_major(zt):
        return zt.transpose(0, 1, 4, 2, 3)

    def heads(zs):
        z = jnp.stack(zs)
        return z.reshape(z.shape[:3] + (n_heads, HEAD_DIM))

    def states(zs):
        z = jnp.stack(zs)
        return z.reshape(z.shape[:2] + (n_groups, n_state))

    return (xp, xs, time_major(kt), time_major(vt), states(hrp), states(hip),
            heads(ksl), heads(vsl), states(hrs), states(his))
```

```python
import functools
import math

import jax
import jax.numpy as jnp
from jax import lax
from jax.experimental import pallas as pl
from jax.experimental.pallas import tpu as pltpu

F32 = jnp.float32
BF16 = jnp.bfloat16

EPS = 1e-6
HEAD_DIM = 64
SSM_GROUP = 16
LANES = 128
ATT_TK = 256
ATT_TQ = 256
ATT_GROUP = 2
DEAD_LOG2 = -160.0
ATT_Q_SCALE = math.log2(math.e) / math.sqrt(HEAD_DIM)
MINUS_INF_LOG2 = -1e30
SSM_CHUNK = 64
ROW_TILE = 1024
MIB = 1024 * 1024


def _params(sem, vmem_mib):
    return pltpu.CompilerParams(dimension_semantics=sem, vmem_limit_bytes=vmem_mib * MIB)


def _rms(x, g):
    ms = jnp.mean(x * x, axis=-1, keepdims=True)
    return x * lax.rsqrt(ms + EPS) * g


def _inproj_body(x_ref, g_ref, w_ref, *refs, n_out, n_alias, heads_t, scale0):
    hn = _rms(x_ref[...], g_ref[...]).astype(BF16)
    out_refs = refs[n_alias:n_alias + n_out]
    width = out_refs[0].shape[-1]
    for n, o_ref in enumerate(out_refs):
        y = jnp.dot(hn, w_ref[:, n * width:(n + 1) * width], preferred_element_type=F32)
        o_ref[...] = (y * scale0 if n == 0 else y).astype(o_ref.dtype)
        if heads_t and n in (1, 2):
            t_ref = refs[n_alias + n_out + n - 1]
            t_ref[0, 0] = y.T.reshape(t_ref.shape[2:])


def _inproj(x2d, g, w_bf, out_dtypes, scale0, stacks=None):
    m, d = x2d.shape
    n_out = len(out_dtypes)
    width = w_bf.shape[1] // n_out
    tm = min(ROW_TILE, m)
    in_specs = [pl.BlockSpec((tm, d), lambda i: (i, 0)),
                pl.BlockSpec((1, d), lambda i: (0, 0)),
                pl.BlockSpec(w_bf.shape, lambda i: (0, 0))]
    out_specs = [pl.BlockSpec((tm, width), lambda i: (i, 0))] * n_out
    out_shape = [jax.ShapeDtypeStruct((m, width), dt) for dt in out_dtypes]
    args = [x2d, g.reshape(1, d), w_bf]
    aliases = {}
    if stacks is not None:
        kt, vt, layer, seq, shape = stacks
        tiles = seq // tm
        assert seq % tm == 0 and shape[2] * shape[3] == width
        stack_spec = pl.BlockSpec((1, 1) + shape[2:4] + (tm,),
                                  lambda i: (layer, i // tiles, 0, 0, i % tiles))
        out_specs += [stack_spec, stack_spec]
        out_shape += [jax.ShapeDtypeStruct(shape, F32)] * 2
        if kt is not None:
            in_specs += [pl.BlockSpec(memory_space=pl.ANY)] * 2
            args += [kt, vt]
            aliases = {3: n_out, 4: n_out + 1}
    return pl.pallas_call(
        functools.partial(_inproj_body, n_out=n_out, n_alias=len(aliases),
                          heads_t=stacks is not None, scale0=scale0),
        grid=(m // tm,),
        in_specs=in_specs,
        out_specs=out_specs,
        out_shape=out_shape,
        input_output_aliases=aliases,
        compiler_params=_params(("parallel",), 56),
        name="inproj",
    )(*args)


def _attn_body(q_ref, k_ref, v_ref, o_ref, unfinished_ref, kb_ref, vb_ref, acc_ref, z_ref, ls_ref,
               after_ref, rs_ref, *, tq, tk, nq, q_off, group):
    if k_ref.dtype == BF16:
        kb_ref, vb_ref = k_ref.at[0], v_ref.at[0]
    else:
        kb_ref[...] = k_ref[0].astype(BF16)
        vb_ref[...] = v_ref[0].astype(BF16)
    rows = 2 * tq
    first_head = lax.broadcasted_iota(jnp.int32, (1, LANES), 1) < HEAD_DIM
    strict = (lax.broadcasted_iota(jnp.int32, (tk, tk), 0)
              > lax.broadcasted_iota(jnp.int32, (tk, tk), 1)).astype(BF16)
    qrow = lax.broadcasted_iota(jnp.int32, (rows, tk), 0) & (tq - 1)
    col = lax.broadcasted_iota(jnp.int32, (rows, tk), 1)
    sign_bit = jnp.uint32(0x80000000)

    def key_rows(j):
        return pl.ds(pl.multiple_of(jnp.maximum(j, 0) * tk, tk), tk)

    def logits(qs, j):
        return lax.dot_general(qs, kb_ref[key_rows(j), :], (((1,), (1,)), ((), ())),
                               preferred_element_type=F32)

    def log_gates(z, mask):
        if mask is not None:
            z = jnp.where(mask, z, MINUS_INF_LOG2)
        neg_abs = lax.bitcast_convert_type(lax.bitcast_convert_type(z, jnp.uint32) | sign_bit, F32)
        ls = jnp.minimum(z, 0.0) - jnp.log2(1.0 + jnp.exp2(neg_abs))
        lr = ls - z
        return ls, lr

    def suffix(lr):
        return jnp.dot(lr.astype(BF16), strict, preferred_element_type=F32)

    def weighted(w, j):
        return jnp.dot(w.astype(BF16), vb_ref[key_rows(j), :], preferred_element_type=F32)

    def middle():
        ls, lr = log_gates(z_ref[...], None)
        ls_ref[...] = ls
        after_ref[...] = suffix(lr)
        rs_ref[...] = jnp.sum(lr, axis=1, keepdims=True)

    def finish(j, carry):
        pv = weighted(jnp.exp2(ls_ref[...] + after_ref[...] + carry), j)
        return pv, carry + rs_ref[...]

    unfinished_ref[...] = jnp.zeros(unfinished_ref.shape, F32)

    def mark_unfinished(flag):
        unfinished_ref[...] = jnp.maximum(unfinished_ref[...], jnp.where(flag, 1.0, 0.0))

    def q_group(g, _):
        tiles = []
        for n in range(group):
            i = g * group + n
            r0 = pl.multiple_of(i * tq, tq)
            q = q_ref[0, pl.ds(r0, tq), :]
            zero = jnp.zeros_like(q)
            qs = jnp.concatenate([jnp.where(first_head, q, zero), jnp.where(first_head, zero, q)],
                                 axis=0)
            t_first = q_off + i * tq
            jmax = (t_first + tq - 1) // tk
            mask = (jmax * tk + col) < (t_first + qrow)
            tiles.append(dict(r0=r0, qs=qs, jmax=jmax, mask=mask,
                              z0=logits(qs, jmax), z1=logits(qs, jmax - 1)))
        for tl in tiles:
            tl["ls0"], lr0 = log_gates(tl.pop("z0"), tl["mask"])
            tl["after0"] = suffix(lr0)
            tl["rs0"] = jnp.sum(lr0, axis=1, keepdims=True)
            tl["ls1"], lr1 = log_gates(tl.pop("z1"), None)
            tl["after1"] = suffix(lr1)
            tl["rs1"] = jnp.sum(lr1, axis=1, keepdims=True)
        for n, tl in enumerate(tiles):
            jmax = tl["jmax"]
            w0 = jnp.exp2(tl["ls0"] + tl["after0"])
            no_tile1 = jnp.where(jmax >= 1, 0.0, MINUS_INF_LOG2)
            w1 = jnp.exp2(tl["ls1"] + tl["after1"] + (tl["rs0"] + no_tile1))
            acc_ref[n] = weighted(w0, jmax) + weighted(w1, jmax - 1)
            tl["carry"] = tl["rs0"] + jnp.where(jmax >= 1, tl["rs1"], 0.0)

        for n, tl in enumerate(tiles):
            qs, jmax, carry = tl["qs"], tl["jmax"], tl["carry"]
            alive = jnp.max(carry) > DEAD_LOG2
            mark_unfinished(jnp.logical_and(jmax < 2, alive))

            @pl.when(jnp.logical_and(jmax >= 2, alive))
            def _(n=n, qs=qs, jmax=jmax, carry=carry):
                z_ref[...] = logits(qs, jmax - 2)
                z_next = logits(qs, jmax - 3)
                middle()
                z_ref[...] = z_next

                def kstep(st):
                    t, carry, _ = st
                    z_next = logits(qs, jmax - t)
                    pv, carry = finish(jmax - t + 2, carry)
                    acc_ref[n] += pv
                    middle()
                    z_ref[...] = z_next
                    return t + 1, carry, jnp.max(carry) > DEAD_LOG2

                st = lax.while_loop(lambda st: jnp.logical_and(st[0] <= jmax + 2, st[2]), kstep,
                                    (jnp.int32(4), carry, jnp.bool_(True)))
                mark_unfinished(st[2])

            o_ref[0, pl.ds(tl["r0"], tq), :] = jnp.where(first_head, acc_ref[n, :tq, :],
                                                         acc_ref[n, tq:, :]).astype(o_ref.dtype)
        return 0

    lax.fori_loop(0, nq // group, q_group, 0)


def _attention(q, k_all, v_all, q_off):
    b, t, c = q.shape
    s = k_all.shape[1]
    tq = min(ATT_TQ, t)
    assert t % tq == 0 and ATT_TK % tq == 0 and q_off % tq == 0 and s % ATT_TK == 0
    nq = t // tq
    group = ATT_GROUP if nq % ATT_GROUP == 0 else 1
    body = functools.partial(_attn_body, tq=tq, tk=ATT_TK, nq=nq, q_off=q_off, group=group)
    qspec = pl.BlockSpec((1, t, LANES), lambda bi, hp: (bi, 0, hp))
    kspec = pl.BlockSpec((1, s, LANES), lambda bi, hp: (bi, 0, hp))
    return pl.pallas_call(
        body,
        grid=(b, c // LANES),
        in_specs=[qspec, kspec, kspec],
        out_specs=[qspec, pl.BlockSpec((1, 1, 8, LANES), lambda bi, hp: (bi, hp, 0, 0))],
        out_shape=[jax.ShapeDtypeStruct((b, t, c), BF16),
                   jax.ShapeDtypeStruct((b, c // LANES, 8, LANES), F32)],
        scratch_shapes=[pltpu.VMEM((s, LANES), BF16), pltpu.VMEM((s, LANES), BF16),
                        pltpu.VMEM((group, 2 * tq, LANES), F32),
                        pltpu.VMEM((2 * tq, ATT_TK), F32),
                        pltpu.VMEM((2 * tq, ATT_TK), F32),
                        pltpu.VMEM((2 * tq, ATT_TK), F32),
                        pltpu.VMEM((2 * tq, 1), F32)],
        compiler_params=_params(("parallel", "parallel"), 48),
        name="sb_attention",
    )(q, k_all, v_all)


def _ssm_body(u_ref, h0r_ref, h0i_ref, bd_ref, cm_ref, a_ref, d_ref, wg_ref, bg_ref,
              y_ref, hr_ref, hi_ref, h_sc, utb_sc, bu_sc, ytb_sc, *, chunk, batch, ntile):
    half = bd_ref.shape[2] // 2

    @pl.when(pl.program_id(0) == 0)
    def _():
        for j in range(ntile):
            h_sc[j, 0] = h0r_ref[:, j * half:(j + 1) * half]
            h_sc[j, 1] = h0i_ref[:, j * half:(j + 1) * half]

    for bi in range(batch):
        for j in range(ntile):
            utb_sc[j, pl.ds(bi, chunk, stride=batch), :] = (
                u_ref[bi, :, j * LANES:(j + 1) * LANES].astype(F32))

    for j in range(ntile):
        bu_sc[...] = jnp.dot(utb_sc[j].astype(BF16), bd_ref[j], preferred_element_type=F32)
        ar = a_ref[j, 0:1, :]
        ai = a_ref[j, 1:2, :]

        def step(t, h):
            hr, hi = h
            r0 = pl.multiple_of(t * batch, batch)
            nr = ar * hr - ai * hi + bu_sc[pl.ds(r0, batch), :half]
            ni = ar * hi + ai * hr + bu_sc[pl.ds(r0, batch), half:]
            bu_sc[pl.ds(r0, batch), :half] = nr
            bu_sc[pl.ds(r0, batch), half:] = ni
            return nr, ni

        hr, hi = lax.fori_loop(0, chunk, step, (h_sc[j, 0], h_sc[j, 1]), unroll=True)
        h_sc[j, 0] = hr
        h_sc[j, 1] = hi
        hr_ref[:, j * half:(j + 1) * half] = hr
        hi_ref[:, j * half:(j + 1) * half] = hi
        ytb_sc[j] = jnp.dot(bu_sc[...].astype(BF16), cm_ref[j], preferred_element_type=F32)

    ch = jnp.concatenate([ytb_sc[j] for j in range(ntile)], axis=1)
    uu = jnp.concatenate([utb_sc[j] for j in range(ntile)], axis=1)
    y = jax.nn.gelu(ch + d_ref[...] * uu)
    gl = jnp.dot(y.astype(BF16), wg_ref[...], preferred_element_type=F32) + bg_ref[...]
    out = y * jax.nn.sigmoid(gl)
    for j in range(ntile):
        ytb_sc[j] = out[:, j * LANES:(j + 1) * LANES]
    for bi in range(batch):
        for j in range(ntile):
            y_ref[bi, :, j * LANES:(j + 1) * LANES] = (
                ytb_sc[j, pl.ds(bi, chunk, stride=batch), :].astype(y_ref.dtype))


def _ssm(u, h0r, h0i, bd, cm, a, d, wg_bf, bg):
    b, t, c = u.shape
    ntile = c // LANES
    states = h0r.shape[1]
    chunk = min(SSM_CHUNK, t)
    body = functools.partial(_ssm_body, chunk=chunk, batch=b, ntile=ntile)
    const = lambda shape: pl.BlockSpec(shape, lambda i: (0,) * len(shape))
    return pl.pallas_call(
        body,
        grid=(t // chunk,),
        in_specs=[pl.BlockSpec((b, chunk, c), lambda i: (0, i, 0)),
                  const((b, states)), const((b, states)),
                  const(bd.shape), const(cm.shape), const(a.shape),
                  const((1, c)), const(wg_bf.shape), const((1, c))],
        out_specs=[pl.BlockSpec((b, chunk, c), lambda i: (0, i, 0)),
                   const((b, states)), const((b, states))],
        out_shape=[jax.ShapeDtypeStruct((b, t, c), BF16),
                   jax.ShapeDtypeStruct((b, states), F32),
                   jax.ShapeDtypeStruct((b, states), F32)],
        scratch_shapes=[pltpu.VMEM((ntile, 2, b, states // ntile), F32),
                        pltpu.VMEM((ntile, chunk * b, LANES), F32),
                        pltpu.VMEM((chunk * b, 2 * states // ntile), F32),
                        pltpu.VMEM((ntile, chunk * b, LANES), F32)],
        compiler_params=_params(("arbitrary",), 48),
        name="s5_branch",
    )(u, h0r, h0i, bd, cm, a, d.reshape(1, c), wg_bf, bg.reshape(1, c))


def _ssm_weights(a_re, a_im, log_dt, b_re, b_im, c_re, c_im, ntile):
    g, p = a_re.shape
    gt = g // ntile
    dt = jnp.exp(log_dt)[:, None]
    mag = jnp.exp(a_re * dt)
    ang = a_im * dt
    ab_re = mag * jnp.cos(ang)
    ab_im = mag * jnp.sin(ang)
    den = a_re * a_re + a_im * a_im
    f_re = ((ab_re - 1.0) * a_re + ab_im * a_im) / den
    f_im = (ab_im * a_re - (ab_re - 1.0) * a_im) / den
    bb_re = f_re[..., None] * b_re - f_im[..., None] * b_im
    bb_im = f_re[..., None] * b_im + f_im[..., None] * b_re
    eye = jnp.eye(gt, dtype=F32)

    def expand(bb):
        bb = bb.reshape(ntile, gt, p, SSM_GROUP)
        return jnp.einsum("jgpc,gh->jgchp", bb, eye).reshape(ntile, gt * SSM_GROUP, gt * p)

    def contract(cc):
        cc = cc.reshape(ntile, gt, SSM_GROUP, p)
        return jnp.einsum("jgcp,gh->jhpgc", cc, eye).reshape(ntile, gt * p, gt * SSM_GROUP)

    bd = jnp.concatenate([expand(bb_re), expand(bb_im)], axis=2).astype(BF16)
    cm = jnp.concatenate([contract(c_re), -contract(c_im)], axis=1).astype(BF16)
    a = jnp.stack([ab_re.reshape(ntile, gt * p), ab_im.reshape(ntile, gt * p)], axis=1)
    return bd, cm, a


def _outproj_body(att_ref, ga_ref, sy_ref, gs_ref, x_ref, gatt_ref, gssm_ref, w_ref, gout_ref,
                  o_ref, *, norm_out):
    ga = ga_ref[...].astype(F32)
    gs = gs_ref[...].astype(F32)
    a = (_rms(att_ref[...].astype(F32), gatt_ref[...]) * (ga * jax.nn.sigmoid(ga))).astype(BF16)
    s = (_rms(sy_ref[...].astype(F32), gssm_ref[...]) * (gs * jax.nn.sigmoid(gs))).astype(BF16)
    d_att = a.shape[1]
    out = (jnp.dot(a, w_ref[:d_att, :], preferred_element_type=F32)
           + jnp.dot(s, w_ref[d_att:, :], preferred_element_type=F32))
    x_new = x_ref[...] + out
    o_ref[...] = _rms(x_new, gout_ref[...]) if norm_out else x_new


def _outproj(att, ga, sy, gs, x2d, g_att, g_ssm, w_bf, g_out, norm_out):
    m, d = x2d.shape
    c = att.shape[1]
    tm = min(ROW_TILE, m)
    half = pl.BlockSpec((tm, c), lambda i: (i, 0))
    return pl.pallas_call(
        functools.partial(_outproj_body, norm_out=norm_out),
        grid=(m // tm,),
        in_specs=[half, half, half, half,
                  pl.BlockSpec((tm, d), lambda i: (i, 0)),
                  pl.BlockSpec((1, c), lambda i: (0, 0)),
                  pl.BlockSpec((1, c), lambda i: (0, 0)),
                  pl.BlockSpec(w_bf.shape, lambda i: (0, 0)),
                  pl.BlockSpec((1, d), lambda i: (0, 0))],
        out_specs=pl.BlockSpec((tm, d), lambda i: (i, 0)),
        out_shape=jax.ShapeDtypeStruct((m, d), F32),
        compiler_params=_params(("parallel",), 48),
        name="outproj",
    )(att, ga, sy, gs, x2d, g_att.reshape(1, c), g_ssm.reshape(1, c), w_bf, g_out.reshape(1, d))


def _cache_rows_body(x_ref, o_ref):
    for h in range(x_ref.shape[3]):
        o_ref[0, :, h * HEAD_DIM:(h + 1) * HEAD_DIM] = x_ref[0, 0, :, h, :]


def _cache_rows(cache, layer):
    _, b, p, h, dh = cache.shape
    rows = min(ATT_TK, p)
    assert p % rows == 0 and dh == HEAD_DIM
    return pl.pallas_call(
        _cache_rows_body,
        grid=(b, p // rows),
        in_specs=[pl.BlockSpec((1, 1, rows, h, dh), lambda bi, j: (layer, bi, j, 0, 0))],
        out_specs=pl.BlockSpec((1, rows, h * dh), lambda bi, j: (bi, j, 0)),
        out_shape=jax.ShapeDtypeStruct((b, p, h * dh), F32),
        compiler_params=_params(("parallel", "parallel"), 32),
        name="cache_rows",
    )(cache)


def _sample_attention(q3, k3, v3, cache_k, cache_v, layer):
    b, t, c = q3.shape
    p = cache_k.shape[2]

    def with_past(k_rows, v_rows):
        n = k_rows.shape[1]
        zeros = jnp.zeros((b, (-(n + t)) % ATT_TK, c), F32)
        return _attention(q3, jnp.concatenate([k_rows, k3, zeros], axis=1),
                          jnp.concatenate([v_rows, v3, zeros], axis=1), n)

    def whole_past():
        return with_past(_cache_rows(cache_k, layer), _cache_rows(cache_v, layer))[0]

    if p <= ATT_TK:
        return whole_past()
    att, unfinished = with_past(cache_k[layer, :, p - ATT_TK:].reshape(b, ATT_TK, c),
                                cache_v[layer, :, p - ATT_TK:].reshape(b, ATT_TK, c))
    return lax.cond(jnp.max(unfinished) > 0.0, whole_past, lambda: att)


def _layer(x, past, stacks, h0_re, h0_im, ln_g, w_in_bf, ssm_w, d_skip, w_glu_bf, b_glu,
           g_att, g_ssm, w_out_bf, g_out, norm_out):
    b, t, d = x.shape
    x2d = x.reshape(b * t, d)
    if stacks is None:
        q, k, v, ga, u, gs = _inproj(x2d, ln_g, w_in_bf, (BF16, F32, F32, BF16, BF16, BF16), ATT_Q_SCALE)
    else:
        stack_shape = (stacks[3], b, w_in_bf.shape[1] // 6 // HEAD_DIM, HEAD_DIM, t)
        q, k, v, ga, u, gs, kt, vt = _inproj(x2d, ln_g, w_in_bf, (BF16,) * 6, ATT_Q_SCALE,
                                             stacks[:3] + (t, stack_shape))
    c = q.shape[1]
    q3, k3, v3 = (z.reshape(b, t, c) for z in (q, k, v))
    if past is None:
        att, _ = _attention(q3, k3, v3, 0)
    else:
        att = _sample_attention(q3, k3, v3, *past)
    bd, cm, a = ssm_w
    states = a.shape[0] * a.shape[2]
    if h0_re is None:
        h0_re = jnp.zeros((b, states), F32)
        h0_im = jnp.zeros((b, states), F32)
    sy, h_re, h_im = _ssm(u.reshape(b, t, c), h0_re.reshape(b, states), h0_im.reshape(b, states),
                          bd, cm, a, d_skip, w_glu_bf, b_glu)
    x_new = _outproj(att.reshape(b * t, c), ga, sy.reshape(b * t, c), gs, x2d, g_att, g_ssm, w_out_bf,
                     g_out, norm_out)
    return x_new.reshape(b, t, d), ((k3, v3) if stacks is None else (kt, vt)), h_re, h_im


def kernel(x_prompt, x_sample, cache_k, cache_v, state_ssm_re, state_ssm_im, ln_g, w_in, ssm_a_re, ssm_a_im, ssm_log_dt, ssm_b_re, ssm_b_im, ssm_c_re, ssm_c_im, ssm_d, w_glu, b_glu, g_att, g_ssm, w_out, final_g):
    depth = w_in.shape[0]
    n_groups, n_state = ssm_a_re.shape[1:]
    c = ssm_d.shape[1]
    n_heads = c // HEAD_DIM
    xp, xs = x_prompt, x_sample
    kt = vt = None
    hrp, hip, ksl, vsl, hrs, his = ([] for _ in range(6))
    for l in range(depth):
        ssm_w = _ssm_weights(ssm_a_re[l], ssm_a_im[l], ssm_log_dt[l], ssm_b_re[l], ssm_b_im[l],
                             ssm_c_re[l], ssm_c_im[l], c // LANES)
        w = (ln_g[l], w_in[l].astype(BF16), ssm_w, ssm_d[l], w_glu[l].astype(BF16), b_glu[l],
             g_att[l], g_ssm[l], w_out[l].astype(BF16), final_g, l == depth - 1)
        xp, (kt, vt), r1, i1 = _layer(xp, None, (kt, vt, l, depth), None, None, *w)
        xs, (k2, v2), r2, i2 = _layer(xs, (cache_k, cache_v, l), None, state_ssm_re[l], state_ssm_im[l], *w)
        for lst, val in zip((hrp, hip, ksl, vsl, hrs, his), (r1, i1, k2, v2, r2, i2)):
            lst.append(val)

    def time_major(zt):
        return zt.transpose(0, 1, 4, 2, 3)

    def heads(zs):
        z = jnp.stack(zs)
        return z.reshape(z.shape[:3] + (n_heads, HEAD_DIM))

    def states(zs):
        z = jnp.stack(zs)
        return z.reshape(z.shape[:2] + (n_groups, n_state))

    return (xp, xs, time_major(kt), time_major(vt), states(hrp), states(hip),
            heads(ksl), heads(vsl), states(hrs), states(his))
```
